```python
import math
import jax, jax.numpy as jnp
from jax import lax
import numpy as np

D_MODEL = 1024
BATCH = 4
SEQ = 8192
DEPTH = 1
DEC_BATCH = 32
DEC_SEQ = 4
PAST_LEN = 16384
PAGE_SIZE = 128

A_GROUPS = ((128, 1), (512, 4), (2048, 16))
N_A_GROUPS = len(A_GROUPS)
A_HEADS = 8
A_HEAD_DIM = 64
A_OUT = A_HEADS * A_HEAD_DIM
B_D_INNER = (3 * D_MODEL) // 2
B_HEAD_DIM = 64
B_HEADS = B_D_INNER // B_HEAD_DIM
B_GROUPS = 4
B_D_STATE = 128
B_CONV = 4
B_CHUNK = 128
B_CONV_DIM = B_D_INNER + 2 * B_GROUPS * B_D_STATE
D_FF = ((8 * D_MODEL // 3 + 127) // 128) * 128
RMS_EPS = 1e-6

OFF_A = 0
OFF_Z = OFF_A + N_A_GROUPS * 3 * A_OUT
OFF_XBC = OFF_Z + B_D_INNER
OFF_DT = OFF_XBC + B_CONV_DIM
OFF_GATE = OFF_DT + B_HEADS
N_IN_COLS = OFF_GATE + 2 * D_MODEL

kernel_name = "dilated_attn_ssd_gated_hybrid_step"


def rmsnorm(x, g):
    xf = x.astype(jnp.float32)
    y = xf * lax.rsqrt(jnp.mean(xf * xf, axis=-1, keepdims=True) + RMS_EPS)
    return (y * g.astype(jnp.float32)).astype(x.dtype)


def swiglu(x, wg, wu, wd):
    return (jax.nn.silu(x @ wg) * (x @ wu)) @ wd


def dilated_attn_prompt(q, k, v, window, dil):
    b, s, h, e = q.shape
    nw = window // dil
    ls = s // dil
    nb = -(-ls // nw)
    pad = nb * nw - ls

    def to_blocks(t):
        t = t.reshape(b, ls, dil, h, e).transpose(0, 2, 1, 3, 4)
        t = jnp.pad(t, ((0, 0), (0, 0), (0, pad), (0, 0), (0, 0)))
        return t.reshape(b, dil, nb, nw, h, e)

    def with_prev(t):
        prev = jnp.pad(t, ((0, 0), (0, 0), (1, 0), (0, 0), (0, 0), (0, 0)))[:, :, :nb]
        return jnp.concatenate([prev, t], axis=3)

    qb = to_blocks(q)
    kb = with_prev(to_blocks(k))
    vb = with_prev(to_blocks(v))
    scores = jnp.einsum("brnqhe,brnkhe->brnhqk", qb, kb).astype(jnp.float32) * (A_HEAD_DIM ** -0.5)
    qi = jnp.arange(nw)[:, None]
    kj = jnp.arange(2 * nw)[None, :]
    dist = qi + nw - kj
    blk = jnp.arange(nb)[:, None, None]
    valid = (dist >= 0) & (dist <= nw) & ((blk > 0) | (kj >= nw))
    scores = jnp.where(valid[None, None, :, None], scores, -jnp.inf)
    m = jnp.max(scores, axis=-1, keepdims=True)
    pr = jnp.exp(scores - m)
    den = jnp.sum(pr, axis=-1)
    o = jnp.einsum("brnhqk,brnkhe->brnqhe", pr, vb.astype(jnp.float32))
    o = o / jnp.moveaxis(den, -1, -2)[..., None]
    lse = jnp.moveaxis(m[..., 0] + jnp.log(den), -1, -2)

    def from_blocks(t):
        t = t.reshape((b, dil, nb * nw) + t.shape[4:])[:, :, :ls]
        t = jnp.moveaxis(t, 1, 2)
        return t.reshape((b, s) + t.shape[3:])

    return from_blocks(o), from_blocks(lse)


def dilated_attn_sample(q, k, v, kv_buf, window, dil):
    b, l, h, e = q.shape
    lb = kv_buf.shape[1]
    nw = window // dil
    kv = jnp.concatenate([kv_buf, jnp.stack([k, v], axis=2).astype(kv_buf.dtype)], axis=1)
    idx = lb + jnp.arange(l)[:, None] - dil * jnp.arange(nw + 1)[None, :]
    valid = idx >= 0
    idx = jnp.maximum(idx, 0)
    kg = kv[:, :, 0][:, idx]
    vg = kv[:, :, 1][:, idx]
    scores = jnp.einsum("blhe,bljhe->blhj", q, kg).astype(jnp.float32) * (A_HEAD_DIM ** -0.5)
    scores = jnp.where(valid[None, :, None, :], scores, -jnp.inf)
    m = jnp.max(scores, axis=-1, keepdims=True)
    pr = jnp.exp(scores - m)
    den = jnp.sum(pr, axis=-1)
    o = jnp.einsum("blhj,bljhe->blhe", pr, vg.astype(jnp.float32)) / den[..., None]
    lse = m[..., 0] + jnp.log(den)
    new_buf = kv[:, -min(window, lb + l):]
    return o, lse, new_buf


def causal_conv(xbc, buf, w, bias):
    l = xbc.shape[1]
    xp = jnp.concatenate([buf.astype(xbc.dtype), xbc], axis=1)
    y = bias
    for tap in range(B_CONV):
        y = y + xp[:, tap:tap + l] * w[tap]
    return jax.nn.silu(y), xp[:, -(B_CONV - 1):]


def ssd_chunked(x, dt, a_neg, bm, cm, init_state):
    f32 = jnp.float32
    b, l, nh, hp = x.shape
    rep = nh // B_GROUPS
    t = min(B_CHUNK, l)
    nc = -(-l // t)
    pad = nc * t - l

    def padt(a):
        return jnp.pad(a, [(0, 0), (0, pad)] + [(0, 0)] * (a.ndim - 2))

    xd = padt((x.astype(f32) * dt[..., None])).reshape(b, nc, t, B_GROUPS, rep, hp)
    da = padt(dt * a_neg).reshape(b, nc, t, B_GROUPS, rep)
    bc = padt(bm.astype(f32)).reshape(b, nc, t, B_GROUPS, B_D_STATE)
    cc = padt(cm.astype(f32)).reshape(b, nc, t, B_GROUPS, B_D_STATE)
    cs = jnp.cumsum(da, axis=2)
    seg = cs[:, :, :, None] - cs[:, :, None, :]
    causal = jnp.tril(jnp.ones((t, t), bool))[None, None, :, :, None, None]
    lmat = jnp.exp(jnp.where(causal, seg, -jnp.inf))
    cb = jnp.einsum("bcign,bcjgn->bcijg", cc, bc)
    y_diag = jnp.einsum("bcijgr,bcjgrp->bcigrp", cb[..., None] * lmat, xd)
    decay_to_end = jnp.exp(cs[:, :, -1:] - cs)
    chunk_states = jnp.einsum("bcjgn,bcjgrp->bcgrpn", bc, xd * decay_to_end[..., None])
    chunk_decay = jnp.exp(cs[:, :, -1])

    def step(s, inp):
        dec, st = inp
        return s * dec[..., None, None] + st, s

    s0 = init_state.astype(f32).reshape(b, B_GROUPS, rep, hp, B_D_STATE)
    final, prev = lax.scan(step, s0, (jnp.moveaxis(chunk_decay, 1, 0), jnp.moveaxis(chunk_states, 1, 0)))
    prev = jnp.moveaxis(prev, 0, 1)
    y_off = jnp.einsum("bcign,bcgrpn->bcigrp", cc, prev) * jnp.exp(cs)[..., None]
    y = (y_diag + y_off).reshape(b, nc * t, nh, hp)[:, :l]
    return y, final.reshape(b, nh, hp, B_D_STATE)


def token_mixer(hn, kv_bufs, conv_buf, ssm_state, p):
    f32 = jnp.float32
    b, l, _ = hn.shape
    proj = hn @ p["w_in"]
    qkv = proj[..., OFF_A:OFF_Z].reshape(b, l, N_A_GROUPS, 3, A_HEADS, A_HEAD_DIM)
    outs, lses, new_kv = [], [], []
    for g, (window, dil) in enumerate(A_GROUPS):
        q, k, v = qkv[:, :, g, 0], qkv[:, :, g, 1], qkv[:, :, g, 2]
        if kv_bufs is None:
            o, lse = dilated_attn_prompt(q, k, v, window, dil)
            kv_new = jnp.stack([k, v], axis=2)[:, -min(window, l):]
        else:
            o, lse, kv_new = dilated_attn_sample(q, k, v, kv_bufs[g], window, dil)
        outs.append(o)
        lses.append(lse)
        new_kv.append(kv_new)
    alpha = jax.nn.softmax(jnp.stack(lses, axis=2), axis=2)
    o_a = jnp.sum(alpha[..., None] * jnp.stack(outs, axis=2), axis=2)
    o_a = o_a.reshape(b, l, A_OUT).astype(hn.dtype) @ p["w_branch_a"]
    z = proj[..., OFF_Z:OFF_XBC]
    xbc, new_conv = causal_conv(proj[..., OFF_XBC:OFF_DT], conv_buf, p["conv_w"], p["conv_b"])
    xs = xbc[..., :B_D_INNER].reshape(b, l, B_HEADS, B_HEAD_DIM)
    bm = xbc[..., B_D_INNER:B_D_INNER + B_GROUPS * B_D_STATE].reshape(b, l, B_GROUPS, B_D_STATE)
    cm = xbc[..., B_D_INNER + B_GROUPS * B_D_STATE:].reshape(b, l, B_GROUPS, B_D_STATE)
    dt = jax.nn.softplus(proj[..., OFF_DT:OFF_GATE].astype(f32) + p["dt_bias"].astype(f32))
    a_neg = -jnp.exp(p["a_log"].astype(f32))
    y, new_ssm = ssd_chunked(xs, dt, a_neg, bm, cm, ssm_state)
    y = y + xs.astype(f32) * p["d_skip"].astype(f32)[:, None]
    y = y.reshape(b, l, B_D_INNER) * jax.nn.silu(z.astype(f32))
    yg = y.reshape(b, l, B_GROUPS, B_D_INNER // B_GROUPS)
    yg = yg * lax.rsqrt(jnp.mean(yg * yg, axis=-1, keepdims=True) + RMS_EPS)
    y = yg.reshape(b, l, B_D_INNER) * p["ssd_norm_w"].astype(f32)
    o_b = y.astype(hn.dtype) @ p["w_branch_b"]
    gates = jax.nn.sigmoid(proj[..., OFF_GATE:].astype(f32))
    merged = gates[..., :D_MODEL] * o_a.astype(f32) + gates[..., D_MODEL:] * o_b.astype(f32)
    out = merged.astype(hn.dtype) @ p["w_out"]
    return out, new_kv, new_conv, new_ssm.astype(ssm_state.dtype)


def trunk_layer(x, kv_bufs, conv_buf, ssm_state, p):
    f1 = swiglu(rmsnorm(x, p["g_pre_ffn1"]), p["ffn1_gate"], p["ffn1_up"], p["ffn1_down"])
    h = x + 0.5 * rmsnorm(f1, p["g_post_ffn1"])
    mix, new_kv, new_conv, new_ssm = token_mixer(rmsnorm(h, p["g_pre_mix"]), kv_bufs, conv_buf, ssm_state, p)
    h = h + rmsnorm(mix, p["g_post_mix"])
    f2 = swiglu(rmsnorm(h, p["g_pre_ffn2"]), p["ffn2_gate"], p["ffn2_up"], p["ffn2_down"])
    h = h + 0.5 * rmsnorm(f2, p["g_post_ffn2"])
    return h, new_kv, new_conv, new_ssm


def setup_inputs(seed: int = 0) -> dict:
    key = jax.random.key(seed)
    ks = iter(jax.random.split(key, 48))
    f32 = jnp.float32
    D = D_MODEL

    def nrm(shape, scale=1.0):
        return jax.random.normal(next(ks), shape, f32) * scale

    def gain(n):
        return 1.0 + nrm((DEPTH, n), 0.02)

    inp = {}
    inp["x_prompt"] = nrm((BATCH, SEQ, D))
    inp["x_sample"] = nrm((DEC_BATCH, DEC_SEQ, D))
    for window, _ in A_GROUPS:
        inp["cache_kv_w" + str(window)] = nrm((DEPTH, DEC_BATCH, min(window, PAST_LEN), 2, A_HEADS, A_HEAD_DIM))
    inp["state_conv"] = nrm((DEPTH, DEC_BATCH, B_CONV - 1, B_CONV_DIM))
    inp["state_ssm"] = nrm((DEPTH, DEC_BATCH, B_HEADS, B_HEAD_DIM, B_D_STATE), 0.1)
    inp["w_in"] = nrm((DEPTH, D, N_IN_COLS), D ** -0.5)
    inp["conv_w"] = nrm((DEPTH, B_CONV, B_CONV_DIM), B_CONV ** -0.5)
    inp["conv_b"] = nrm((DEPTH, B_CONV_DIM), 0.01)
    u = jax.random.uniform(next(ks), (DEPTH, B_HEADS), f32)
    dt0 = jnp.exp(u * (math.log(0.1) - math.log(0.001)) + math.log(0.001))
    inp["dt_bias"] = dt0 + jnp.log(-jnp.expm1(-dt0))
    inp["a_log"] = jnp.log(jax.random.uniform(next(ks), (DEPTH, B_HEADS), f32, 1.0, 16.0))
    inp["d_skip"] = 1.0 + nrm((DEPTH, B_HEADS), 0.1)
    inp["ssd_norm_w"] = gain(B_D_INNER)
    inp["w_branch_a"] = nrm((DEPTH, A_OUT, D), A_OUT ** -0.5)
    inp["w_branch_b"] = nrm((DEPTH, B_D_INNER, D), B_D_INNER ** -0.5)
    inp["w_out"] = nrm((DEPTH, D, D), D ** -0.5)
    inp["ffn1_gate"] = nrm((DEPTH, D, D_FF), D ** -0.5)
    inp["ffn1_up"] = nrm((DEPTH, D, D_FF), D ** -0.5)
    inp["ffn1_down"] = nrm((DEPTH, D_FF, D), D_FF ** -0.5)
    inp["ffn2_gate"] = nrm((DEPTH, D, D_FF), D ** -0.5)
    inp["ffn2_up"] = nrm((DEPTH, D, D_FF), D ** -0.5)
    inp["ffn2_down"] = nrm((DEPTH, D_FF, D), D_FF ** -0.5)
    inp["g_pre_ffn1"] = gain(D)
    inp["g_post_ffn1"] = gain(D)
    inp["g_pre_mix"] = gain(D)
    inp["g_post_mix"] = gain(D)
    inp["g_pre_ffn2"] = gain(D)
    inp["g_post_ffn2"] = gain(D)
    return inp


def reference(x_prompt, x_sample, cache_kv_w128, cache_kv_w512, cache_kv_w2048, state_conv, state_ssm,
              w_in, conv_w, conv_b, dt_bias, a_log, d_skip, ssd_norm_w, w_branch_a, w_branch_b, w_out,
              ffn1_gate, ffn1_up, ffn1_down, ffn2_gate, ffn2_up, ffn2_down,
              g_pre_ffn1, g_post_ffn1, g_pre_mix, g_post_mix, g_pre_ffn2, g_post_ffn2):
    y_p, y_s = x_prompt, x_sample
    pk128, pk512, pk2048, pconv, pssm = [], [], [], [], []
    sk128, sk512, sk2048, sconv, sssm = [], [], [], [], []
    nbp = x_prompt.shape[0]
    for layer in range(DEPTH):
        p = {"w_in": w_in[layer], "conv_w": conv_w[layer], "conv_b": conv_b[layer],
             "dt_bias": dt_bias[layer], "a_log": a_log[layer], "d_skip": d_skip[layer],
             "ssd_norm_w": ssd_norm_w[layer], "w_branch_a": w_branch_a[layer],
             "w_branch_b": w_branch_b[layer], "w_out": w_out[layer],
             "ffn1_gate": ffn1_gate[layer], "ffn1_up": ffn1_up[layer], "ffn1_down": ffn1_down[layer],
             "ffn2_gate": ffn2_gate[layer], "ffn2_up": ffn2_up[layer], "ffn2_down": ffn2_down[layer],
             "g_pre_ffn1": g_pre_ffn1[layer], "g_post_ffn1": g_post_ffn1[layer],
             "g_pre_mix": g_pre_mix[layer], "g_post_mix": g_post_mix[layer],
             "g_pre_ffn2": g_pre_ffn2[layer], "g_post_ffn2": g_post_ffn2[layer]}
        conv0 = jnp.zeros((nbp, B_CONV - 1, B_CONV_DIM), x_prompt.dtype)
        ssm0 = jnp.zeros((nbp, B_HEADS, B_HEAD_DIM, B_D_STATE), jnp.float32)
        y_p, kv_p, conv_p, ssm_p = trunk_layer(y_p, None, conv0, ssm0, p)
        bufs = (cache_kv_w128[layer], cache_kv_w512[layer], cache_kv_w2048[layer])
        y_s, kv_s, conv_s, ssm_s = trunk_layer(y_s, bufs, state_conv[layer], state_ssm[layer], p)
        pk128.append(kv_p[0]); pk512.append(kv_p[1]); pk2048.append(kv_p[2])
        pconv.append(conv_p); pssm.append(ssm_p)
        sk128.append(kv_s[0]); sk512.append(kv_s[1]); sk2048.append(kv_s[2])
        sconv.append(conv_s); sssm.append(ssm_s)
    return (y_p, y_s,
            jnp.stack(pk128), jnp.stack(pk512), jnp.stack(pk2048), jnp.stack(pconv), jnp.stack(pssm),
            jnp.stack(sk128), jnp.stack(sk512), jnp.stack(sk2048), jnp.stack(sconv), jnp.stack(sssm))
```

```python
import functools

import jax
import jax.numpy as jnp
import numpy as np
from jax import lax
from jax.experimental import pallas as pl
from jax.experimental.pallas import tpu as pltpu

F32 = jnp.float32
BF16 = jnp.bfloat16

D_MODEL = 1024
A_GROUPS = ((128, 1), (512, 4), (2048, 16))
N_A_GROUPS = len(A_GROUPS)
A_HEADS = 8
A_HEAD_DIM = 64
A_OUT = A_HEADS * A_HEAD_DIM
A_BAND = 128
B_D_INNER = 1536
B_HEAD_DIM = 64
B_HEADS = B_D_INNER // B_HEAD_DIM
B_GROUPS = 4
B_GROUP_HEADS = B_HEADS // B_GROUPS
B_GROUP_DIM = B_D_INNER // B_GROUPS
B_D_STATE = 128
B_CONV = 4
B_CHUNK = 128
B_CONV_DIM = B_D_INNER + 2 * B_GROUPS * B_D_STATE
D_FF = 2816
RMS_EPS = 1e-6

OFF_Z = N_A_GROUPS * 3 * A_OUT
OFF_XBC = OFF_Z + B_D_INNER
OFF_DT = OFF_XBC + B_CONV_DIM
OFF_GATE = OFF_DT + B_HEADS

LANES = 128
DT_PAD = LANES
OE_COLS = A_OUT + LANES
FF_CHUNK = D_FF // 2
VMEM_LIMIT = 56 * 1024 * 1024
NEG_INF = float("-inf")


def _const_spec(shape):
    nd = len(shape)
    return pl.BlockSpec(shape, lambda *_: (0,) * nd, pipeline_mode=pl.Buffered(1))


def _params(sem):
    return pltpu.CompilerParams(dimension_semantics=sem, vmem_limit_bytes=VMEM_LIMIT)


def _rms(x, g):
    return x * lax.rsqrt(jnp.mean(x * x, axis=-1, keepdims=True) + RMS_EPS) * g


def _silu(x):
    return x * jax.nn.sigmoid(x)


def _dot(a, b):
    return jnp.dot(a, b, preferred_element_type=F32)


def _dot_nt(a, b):
    return lax.dot_general(a, b, (((1,), (1,)), ((), ())), preferred_element_type=F32)


def _dot_tn(a, b):
    return lax.dot_general(a, b, (((0,), (0,)), ((), ())), preferred_element_type=F32)


def _split_bf16(a, parts):
    out = []
    r = a
    for i in range(parts):
        p = r.astype(BF16)
        out.append(p)
        if i + 1 < parts:
            r = r - p.astype(F32)
    return out


def _expand(a, e_ref):
    e = e_ref[...]
    hi, lo = _split_bf16(a, 2)
    return _dot(hi, e) + _dot(lo, e)


def _ffn_kernel(x_ref, gpre_ref, gpost_ref, gnext_ref, wg_ref, wu_ref, wd_ref, *rest, emit_next):
    if emit_next:
        h_ref, hn_ref, a_scr = rest
    else:
        h_ref, a_scr = rest
    x = x_ref[...]
    xn = _rms(x, gpre_ref[...]).astype(BF16)
    for c in range(D_FF // FF_CHUNK):
        sl = slice(c * FF_CHUNK, (c + 1) * FF_CHUNK)
        g = _dot(xn, wg_ref[:, sl])
        u = _dot(xn, wu_ref[:, sl])
        a_scr[:, sl] = (_silu(g) * u).astype(BF16)
    f = _dot(a_scr[...], wd_ref[...])
    h = x + 0.5 * _rms(f, gpost_ref[...])
    h_ref[...] = h
    if emit_next:
        hn_ref[...] = _rms(h, gnext_ref[...]).astype(BF16)


def _ffn_call(x, g_pre, g_post, g_next, wg, wu, wd, tm):
    t = x.shape[0]
    emit_next = g_next is not None
    row = lambda i: (i, 0)
    out_shape = [jax.ShapeDtypeStruct((t, D_MODEL), F32)]
    out_specs = [pl.BlockSpec((tm, D_MODEL), row)]
    if emit_next:
        out_shape.append(jax.ShapeDtypeStruct((t, D_MODEL), BF16))
        out_specs.append(pl.BlockSpec((tm, D_MODEL), row))
    return pl.pallas_call(
        functools.partial(_ffn_kernel, emit_next=emit_next),
        grid=(t // tm,),
        in_specs=[pl.BlockSpec((tm, D_MODEL), row),
                  _const_spec((1, D_MODEL)), _const_spec((1, D_MODEL)), _const_spec((1, D_MODEL)),
                  _const_spec((D_MODEL, D_FF)), _const_spec((D_MODEL, D_FF)), _const_spec((D_FF, D_MODEL))],
        out_specs=out_specs,
        out_shape=out_shape,
        scratch_shapes=[pltpu.VMEM((tm, D_FF), BF16)],
        compiler_params=_params(("parallel",)),
        name="ffn_next" if emit_next else "ffn",
    )(x, g_pre, g_post, g_next if emit_next else g_post, wg, wu, wd)


PROJ_CHUNK = 1536


def _proj_kernel(x_ref, w_ref, *o_refs, splits):
    x = x_ref[...]
    for o_ref, (a, b) in zip(o_refs, splits):
        for c0 in range(a, b, PROJ_CHUNK):
            c1 = min(b, c0 + PROJ_CHUNK)
            o_ref[:, c0 - a:c1 - a] = _dot(x, w_ref[:, c0:c1]).astype(o_ref.dtype)


def _proj_call(x, w, splits, dtypes, tm, name, rows=None, row_block=None):
    t, k = x.shape
    rows = t if rows is None else rows
    row_block = (lambda i: i) if row_block is None else row_block
    return pl.pallas_call(
        functools.partial(_proj_kernel, splits=splits),
        grid=(rows // tm,),
        in_specs=[pl.BlockSpec((tm, k), lambda i: (row_block(i), 0)), _const_spec(w.shape)],
        out_specs=[pl.BlockSpec((tm, b - a), lambda i: (i, 0)) for a, b in splits],
        out_shape=[jax.ShapeDtypeStruct((rows, b - a), dt) for (a, b), dt in zip(splits, dtypes)],
        compiler_params=_params(("parallel",)),
        name=name,
    )(x, w)


def _attn_prompt_kernel(q_ref, kp_ref, kc_ref, vp_ref, vc_ref, o_ref, *, tq):
    n = pl.program_id(2)
    tk = tq + A_BAND
    q = q_ref[0]
    kp, kc, vp, vc = kp_ref[0], kc_ref[0], vp_ref[0], vc_ref[0]
    qi = lax.broadcasted_iota(jnp.int32, (tq, tk), 0)
    kj = lax.broadcasted_iota(jnp.int32, (tq, tk), 1)
    dist = qi + A_BAND - kj
    valid = (dist >= 0) & (dist <= A_BAND) & ((kj >= A_BAND) | (n > 0))
    lane = lax.broadcasted_iota(jnp.int32, (tq, LANES), 1)
    lse_blk = jnp.zeros((tq, LANES), F32)
    for h in range(A_HEADS):
        sl = slice(h * A_HEAD_DIM, (h + 1) * A_HEAD_DIM)
        k_h = jnp.concatenate([kp[:, sl], kc[:, sl]], axis=0)
        v_h = jnp.concatenate([vp[:, sl], vc[:, sl]], axis=0)
        s = _dot_nt(q[:, sl], k_h) * (A_HEAD_DIM ** -0.5)
        s = jnp.where(valid, s, NEG_INF)
        m = jnp.max(s, axis=-1, keepdims=True)
        p = jnp.exp(s - m)
        den = jnp.sum(p, axis=-1, keepdims=True)
        o_ref[0, :, sl] = _dot(p.astype(BF16), v_h) / den
        lse_blk = jnp.where(lane == h, m + jnp.log(den), lse_blk)
    o_ref[0, :, A_OUT:] = lse_blk


def _attn_prompt_call(qkv, group, tq):
    b, s, ncol = qkv.shape
    _, dil = A_GROUPS[group]
    ls = s // dil
    tq = min(tq, ls)
    per = tq // A_BAND
    blocks = ncol // A_OUT
    view = qkv.reshape(b, ls, dil * ncol)
    col = 3 * group

    def cur(j):
        return lambda bi, r, n: (bi, n, r * blocks + col + j)

    def prev(j):
        return lambda bi, r, n: (bi, jnp.maximum(n * per - 1, 0), r * blocks + col + j)

    out = pl.pallas_call(
        functools.partial(_attn_prompt_kernel, tq=tq),
        grid=(b, dil, ls // tq),
        in_specs=[pl.BlockSpec((1, tq, A_OUT), cur(0)),
                  pl.BlockSpec((1, A_BAND, A_OUT), prev(1)), pl.BlockSpec((1, tq, A_OUT), cur(1)),
                  pl.BlockSpec((1, A_BAND, A_OUT), prev(2)), pl.BlockSpec((1, tq, A_OUT), cur(2))],
        out_specs=pl.BlockSpec((1, tq, OE_COLS), lambda bi, r, n: (bi, n, r)),
        out_shape=jax.ShapeDtypeStruct((b, ls, dil * OE_COLS), F32),
        compiler_params=_params(("parallel", "parallel", "arbitrary")),
        name=f"attn_prompt_g{group}",
    )(view, view, view, view, view)
    return out.reshape(b, s, OE_COLS)


S_ROWS = 128
S_PAD = 128
NEW_ROWS = 16


def _attn_sample_kernel(cache_ref, kvnew_ref, q_ref, newbuf_ref, o_ref, lse_ref, kb_scr, vb_scr, *, window, dil, n_new):
    w = window
    newbuf_ref[0, 0:w - n_new, :] = cache_ref[0, n_new:w, :]
    newbuf_ref[0, w - n_new:w, :] = kvnew_ref[0, 0:n_new, :]
    kb_scr[0:w, :] = cache_ref[0, :, 0:A_OUT].astype(BF16)
    vb_scr[0:w, :] = cache_ref[0, :, A_OUT:].astype(BF16)
    kb_scr[w:w + NEW_ROWS, :] = kvnew_ref[0, :, 0:A_OUT].astype(BF16)
    vb_scr[w:w + NEW_ROWS, :] = kvnew_ref[0, :, A_OUT:].astype(BF16)
    kb_scr[w + NEW_ROWS:, :] = jnp.zeros((S_PAD - NEW_ROWS, A_OUT), BF16)
    vb_scr[w + NEW_ROWS:, :] = jnp.zeros((S_PAD - NEW_ROWS, A_OUT), BF16)

    r = lax.broadcasted_iota(jnp.int32, (S_ROWS, A_OUT), 0)
    c = lax.broadcasted_iota(jnp.int32, (S_ROWS, A_OUT), 1)
    head_lanes = (r // 8) == (c // A_HEAD_DIM)
    qm = jnp.where(head_lanes, q_ref[0], 0.0).astype(BF16)
    s = _dot_nt(qm, kb_scr[...]) * (A_HEAD_DIM ** -0.5)
    rr = lax.broadcasted_iota(jnp.int32, s.shape, 0)
    pos = lax.broadcasted_iota(jnp.int32, s.shape, 1)
    d = w + (rr & (n_new - 1)) - pos
    valid = (d >= 0) & (d <= window) & ((d & (dil - 1)) == 0)
    s = jnp.where(valid, s, NEG_INF)
    m = jnp.max(s, axis=-1, keepdims=True)
    p = jnp.exp(s - m)
    den = jnp.sum(p, axis=-1, keepdims=True)
    o = jnp.where(head_lanes, _dot(p.astype(BF16), vb_scr[...]) / den, 0.0)
    lse = m + jnp.log(den)
    r2 = lax.broadcasted_iota(jnp.int32, (S_ROWS, LANES), 0)
    c2 = lax.broadcasted_iota(jnp.int32, (S_ROWS, LANES), 1)
    lse_sel = jnp.where((r2 // 8) == c2, lse, 0.0)
    o_acc = o[0:8]
    lse_acc = lse_sel[0:8]
    for h in range(1, A_HEADS):
        o_acc = o_acc + o[8 * h:8 * h + 8]
        lse_acc = lse_acc + lse_sel[8 * h:8 * h + 8]
    o_ref[0] = o_acc
    lse_ref[0] = lse_acc


def _attn_sample_call(cache, q, k, v, group):
    b, lb, _ = cache.shape
    window, dil = A_GROUPS[group]
    n_new = q.shape[1]
    assert lb == window and n_new == 4
    kvnew = jnp.pad(jnp.concatenate([k, v], axis=-1), ((0, 0), (0, NEW_ROWS - n_new), (0, 0)))
    q8 = jnp.pad(q, ((0, 0), (0, 8 - n_new), (0, 0)))
    q_rows = jnp.pad(jnp.tile(q8, (1, A_HEADS, 1)), ((0, 0), (0, S_ROWS - 8 * A_HEADS), (0, 0)))
    batch = lambda i: (i, 0, 0)
    newbuf, o, lse = pl.pallas_call(
        functools.partial(_attn_sample_kernel, window=window, dil=dil, n_new=n_new),
        grid=(b,),
        in_specs=[pl.BlockSpec((1, lb, 2 * A_OUT), batch),
                  pl.BlockSpec((1, NEW_ROWS, 2 * A_OUT), batch),
                  pl.BlockSpec((1, S_ROWS, A_OUT), batch)],
        out_specs=[pl.BlockSpec((1, window, 2 * A_OUT), batch),
                   pl.BlockSpec((1, 8, A_OUT), batch),
                   pl.BlockSpec((1, 8, LANES), batch)],
        out_shape=[jax.ShapeDtypeStruct((b, window, 2 * A_OUT), F32),
                   jax.ShapeDtypeStruct((b, 8, A_OUT), F32),
                   jax.ShapeDtypeStruct((b, 8, LANES), F32)],
        scratch_shapes=[pltpu.VMEM((lb + S_PAD, A_OUT), BF16), pltpu.VMEM((lb + S_PAD, A_OUT), BF16)],
        compiler_params=_params(("parallel",)),
        name=f"attn_sample_g{group}",
    )(cache, kvnew, q_rows)
    return newbuf, o[:, :n_new], lse[:, :n_new]


XP_OFF = 8


def _ssd_kernel(xbc_ref, z_ref, dt_ref, cinit_ref, sinit_ref, convw_ref, convb_ref, dtb_ref, alog_ref,
                dskip_ref, normw_ref, e_ref, yb_ref, convout_ref, ssmout_ref, state_scr, xp_scr, y_scr, *, valid, nc):
    t = B_CHUNK
    c = pl.program_id(1)

    @pl.when(c == 0)
    def _():
        state_scr[...] = sinit_ref[0].reshape(B_D_INNER, B_D_STATE)
        xp_scr[XP_OFF - (B_CONV - 1):XP_OFF, :] = cinit_ref[0]

    xp_scr[XP_OFF:XP_OFF + t, :] = xbc_ref[0]
    conv = convb_ref[...]
    for tap in range(B_CONV):
        lo = XP_OFF - (B_CONV - 1) + tap
        conv = conv + xp_scr[lo:lo + t, :] * convw_ref[tap:tap + 1, :]
    tail = xp_scr[XP_OFF + valid - (B_CONV - 1):XP_OFF + valid, :]
    xp_scr[XP_OFF - (B_CONV - 1):XP_OFF, :] = tail

    xc = _silu(conv)
    xs = xc[:, :B_D_INNER]
    row = lax.broadcasted_iota(jnp.int32, (t, t), 0)
    col = lax.broadcasted_iota(jnp.int32, (t, t), 1)
    causal = row >= col
    x_dt = dt_ref[0] + dtb_ref[...]
    dt = jnp.maximum(x_dt, 0.0) + jnp.log1p(jnp.exp(-jnp.abs(x_dt)))
    if valid < t:
        dt = jnp.where(row < valid, dt, 0.0)
    da = dt * (-jnp.exp(alog_ref[...]))
    tri = jnp.where(causal, 1.0, 0.0).astype(BF16)
    cs = sum(_dot(tri, part) for part in _split_bf16(da, 3))
    cs_t = cs.T
    dt_t = dt.T
    cs_last = cs[t - 1:t, :]
    both = jnp.concatenate([jnp.exp(cs), dt * jnp.exp(cs_last - cs)], axis=0)
    both_x = _expand(both, e_ref)
    ecs_x = both_x[:t]
    xdd = (xs * both_x[t:]).astype(BF16)
    xs_b = xs.astype(BF16)
    dec_t = jnp.exp(cs_t[:, t - 1:t])

    for g in range(B_GROUPS):
        bm_g = xc[:, B_D_INNER + g * B_D_STATE:B_D_INNER + (g + 1) * B_D_STATE].astype(BF16)
        cm_lo = B_D_INNER + (B_GROUPS + g) * B_D_STATE
        cm_g = xc[:, cm_lo:cm_lo + B_D_STATE].astype(BF16)
        cb = _dot_nt(cm_g, bm_g)
        gs = slice(g * B_GROUP_DIM, (g + 1) * B_GROUP_DIM)
        s_g = state_scr[gs, :]
        y_off = _dot_nt(cm_g, s_g.astype(BF16)) * ecs_x[:, gs]
        decs = []
        for r in range(B_GROUP_HEADS):
            h = g * B_GROUP_HEADS + r
            hs = slice(h * B_HEAD_DIM, (h + 1) * B_HEAD_DIM)
            seg = cs[:, h:h + 1] - cs_t[h:h + 1, :]
            lmat = jnp.exp(jnp.where(causal, seg, NEG_INF))
            mat = (cb * lmat * dt_t[h:h + 1, :]).astype(BF16)
            y_scr[:, hs] = _dot(mat, xs_b[:, hs]) + y_off[:, r * B_HEAD_DIM:(r + 1) * B_HEAD_DIM]
            decs.append(jnp.broadcast_to(dec_t[h:h + 1, :], (B_HEAD_DIM, B_D_STATE)))
        state_scr[gs, :] = s_g * jnp.concatenate(decs, axis=0) + _dot_tn(xdd[:, gs], bm_g)

    y = y_scr[...] + xs * dskip_ref[...]
    y = y * _silu(z_ref[0].astype(F32))
    for g in range(B_GROUPS):
        gs = slice(g * B_GROUP_DIM, (g + 1) * B_GROUP_DIM)
        yg = y[:, gs]
        yg = yg * lax.rsqrt(jnp.mean(yg * yg, axis=-1, keepdims=True) + RMS_EPS)
        yb_ref[0, :, gs] = (yg * normw_ref[:, gs]).astype(BF16)

    @pl.when(c == nc - 1)
    def _():
        convout_ref[0] = tail
        ssmout_ref[0] = state_scr[...].reshape(B_HEADS, B_HEAD_DIM, B_D_STATE)


def _ssd_call(xbc, z, dt, conv_init, ssm_init, prm, valid):
    b, l, _ = xbc.shape
    nc = l // B_CHUNK
    assert nc == 1 or valid == B_CHUNK
    tok = lambda bi, c: (bi, c, 0)
    per_b3 = lambda bi, c: (bi, 0, 0)
    per_b4 = lambda bi, c: (bi, 0, 0, 0)
    return pl.pallas_call(
        functools.partial(_ssd_kernel, valid=valid, nc=nc),
        grid=(b, nc),
        in_specs=[pl.BlockSpec((1, B_CHUNK, B_CONV_DIM), tok),
                  pl.BlockSpec((1, B_CHUNK, B_D_INNER), tok),
                  pl.BlockSpec((1, B_CHUNK, DT_PAD), tok),
                  pl.BlockSpec((1, B_CONV - 1, B_CONV_DIM), per_b3),
                  pl.BlockSpec((1, B_HEADS, B_HEAD_DIM, B_D_STATE), per_b4),
                  _const_spec((B_CONV, B_CONV_DIM)), _const_spec((1, B_CONV_DIM)),
                  _const_spec((1, DT_PAD)), _const_spec((1, DT_PAD)),
                  _const_spec((1, B_D_INNER)), _const_spec((1, B_D_INNER)),
                  _const_spec((LANES, B_D_INNER))],
        out_specs=[pl.BlockSpec((1, B_CHUNK, B_D_INNER), tok),
                   pl.BlockSpec((1, B_CONV - 1, B_CONV_DIM), per_b3),
                   pl.BlockSpec((1, B_HEADS, B_HEAD_DIM, B_D_STATE), per_b4)],
        out_shape=[jax.ShapeDtypeStruct((b, l, B_D_INNER), BF16),
                   jax.ShapeDtypeStruct((b, B_CONV - 1, B_CONV_DIM), F32),
                   jax.ShapeDtypeStruct((b, B_HEADS, B_HEAD_DIM, B_D_STATE), F32)],
        scratch_shapes=[pltpu.VMEM((B_D_INNER, B_D_STATE), F32),
                        pltpu.VMEM((XP_OFF + B_CHUNK, B_CONV_DIM), F32),
                        pltpu.VMEM((B_CHUNK, B_D_INNER), F32)],
        compiler_params=_params(("arbitrary", "arbitrary")),
        name="ssd",
    )(xbc, z, dt, conv_init, ssm_init, prm["conv_w"], prm["conv_b"], prm["dt_bias"], prm["a_log"],
      prm["d_skip"], prm["ssd_norm_w"], prm["expand"])


def _merge_kernel(oe0_ref, oe1_ref, oe2_ref, yb_ref, hn_ref, h_ref, wgate_ref, wa_ref, wb_ref, wout_ref,
                  gpost_ref, e_ref, h2_ref):
    oes = (oe0_ref, oe1_ref, oe2_ref)
    lses = [oe[:, A_OUT:] for oe in oes]
    m = jnp.maximum(jnp.maximum(lses[0], lses[1]), lses[2])
    es = [jnp.exp(l - m) for l in lses]
    tot = es[0] + es[1] + es[2]
    o_a = None
    for oe, e in zip(oes, es):
        term = _expand(e / tot, e_ref) * oe[:, :A_OUT]
        o_a = term if o_a is None else o_a + term
    pa = _dot(o_a.astype(BF16), wa_ref[...])
    pb = _dot(yb_ref[...], wb_ref[...])
    gates = jax.nn.sigmoid(_dot(hn_ref[...], wgate_ref[...]))
    merged = gates[:, :D_MODEL] * pa + gates[:, D_MODEL:] * pb
    mix = _dot(merged.astype(BF16), wout_ref[...])
    h2_ref[...] = h_ref[...] + _rms(mix, gpost_ref[...])


def _merge_call(oes, yb, hn, h, prm, tm):
    t = h.shape[0]
    row = lambda i: (i, 0)
    return pl.pallas_call(
        _merge_kernel,
        grid=(t // tm,),
        in_specs=[pl.BlockSpec((tm, OE_COLS), row)] * 3 + [
            pl.BlockSpec((tm, B_D_INNER), row), pl.BlockSpec((tm, D_MODEL), row), pl.BlockSpec((tm, D_MODEL), row),
            _const_spec((D_MODEL, 2 * D_MODEL)), _const_spec((A_OUT, D_MODEL)),
            _const_spec((B_D_INNER, D_MODEL)), _const_spec((D_MODEL, D_MODEL)),
            _const_spec((1, D_MODEL)), _const_spec((LANES, A_OUT))],
        out_specs=pl.BlockSpec((tm, D_MODEL), row),
        out_shape=jax.ShapeDtypeStruct((t, D_MODEL), F32),
        compiler_params=_params(("parallel",)),
        name="merge",
    )(*oes, yb, hn, h, prm["w_gate"], prm["w_branch_a"], prm["w_branch_b"], prm["w_out"],
      prm["g_post_mix"], prm["expand8"])


def _expand_matrix():
    e = np.zeros((LANES, B_D_INNER), np.float32)
    for h in range(B_HEADS):
        e[h, h * B_HEAD_DIM:(h + 1) * B_HEAD_DIM] = 1.0
    return e


def _prepare(w_in, conv_w, conv_b, dt_bias, a_log, d_skip, ssd_norm_w, w_branch_a, w_branch_b, w_out,
             ffn1_gate, ffn1_up, ffn1_down, ffn2_gate, ffn2_up, ffn2_down,
             g_pre_ffn1, g_post_ffn1, g_pre_mix, g_post_mix, g_pre_ffn2, g_post_ffn2):
    lane_pad = lambda v: jnp.pad(v, (0, DT_PAD - v.shape[0]))[None, :]
    w_zxd = jnp.pad(w_in[:, OFF_Z:OFF_GATE], ((0, 0), (0, DT_PAD - B_HEADS)))
    e = _expand_matrix()
    return {
        "w_qkv": w_in[:, :OFF_Z].astype(BF16),
        "w_zxd": w_zxd.astype(BF16),
        "w_gate": w_in[:, OFF_GATE:].astype(BF16),
        "conv_w": conv_w, "conv_b": conv_b[None, :],
        "dt_bias": lane_pad(dt_bias), "a_log": lane_pad(a_log),
        "d_skip": jnp.repeat(d_skip, B_HEAD_DIM)[None, :],
        "ssd_norm_w": ssd_norm_w[None, :],
        "expand": jnp.asarray(e, BF16), "expand8": jnp.asarray(e[:, :A_OUT], BF16),
        "w_branch_a": w_branch_a.astype(BF16), "w_branch_b": w_branch_b.astype(BF16), "w_out": w_out.astype(BF16),
        "ffn1": (ffn1_gate.astype(BF16), ffn1_up.astype(BF16), ffn1_down.astype(BF16)),
        "ffn2": (ffn2_gate.astype(BF16), ffn2_up.astype(BF16), ffn2_down.astype(BF16)),
        "g_pre_ffn1": g_pre_ffn1[None, :], "g_post_ffn1": g_post_ffn1[None, :],
        "g_pre_mix": g_pre_mix[None, :], "g_post_mix": g_post_mix[None, :],
        "g_pre_ffn2": g_pre_ffn2[None, :], "g_post_ffn2": g_post_ffn2[None, :],
    }


ZXD_SPLITS = ((0, B_D_INNER), (B_D_INNER, B_D_INNER + B_CONV_DIM), (B_D_INNER + B_CONV_DIM, B_D_INNER + B_CONV_DIM + DT_PAD))


def _prompt_layer(x, prm, tm, tq):
    b, s, _ = x.shape
    t = b * s
    h1, hn = _ffn_call(x.reshape(t, D_MODEL), prm["g_pre_ffn1"], prm["g_post_ffn1"], prm["g_pre_mix"], *prm["ffn1"], tm)
    (qkv,) = _proj_call(hn, prm["w_qkv"], ((0, OFF_Z),), (BF16,), tm, "proj_qkv")
    z, xbc, dt = _proj_call(hn, prm["w_zxd"], ZXD_SPLITS, (BF16, F32, F32), tm, "proj_zxd")

    kv_tails = []
    for g, (window, _) in enumerate(A_GROUPS):
        rows = min(window, s)
        tt = min(rows, tm)
        per_b, first = rows // tt, (s - rows) // tt
        w_kv = prm["w_qkv"][:, g * 3 * A_OUT + A_OUT:(g + 1) * 3 * A_OUT]
        (kv,) = _proj_call(hn, w_kv, ((0, 2 * A_OUT),), (F32,), tt, f"proj_kv_g{g}", rows=b * rows,
                           row_block=lambda i, per_b=per_b, first=first, sb=s // tt: (i // per_b) * sb + first + i % per_b)
        kv_tails.append(kv.reshape(1, b, rows, 2, A_HEADS, A_HEAD_DIM))

    qkv3 = qkv.reshape(b, s, OFF_Z)
    oes = [_attn_prompt_call(qkv3, g, tq).reshape(t, OE_COLS) for g in range(N_A_GROUPS)]
    yb, conv_out, ssm_out = _ssd_call(
        xbc.reshape(b, s, B_CONV_DIM), z.reshape(b, s, B_D_INNER), dt.reshape(b, s, DT_PAD),
        jnp.zeros((b, B_CONV - 1, B_CONV_DIM), F32), jnp.zeros((b, B_HEADS, B_HEAD_DIM, B_D_STATE), F32),
        prm, B_CHUNK)
    h2 = _merge_call(oes, yb.reshape(t, B_D_INNER), hn, h1, prm, tm)
    (y,) = _ffn_call(h2, prm["g_pre_ffn2"], prm["g_post_ffn2"], None, *prm["ffn2"], tm)
    return y.reshape(b, s, D_MODEL), kv_tails, conv_out[None], ssm_out[None]


def _sample_layer(x, caches, conv_state, ssm_state, prm):
    b, l, _ = x.shape
    t = b * l
    h1, hn = _ffn_call(x.reshape(t, D_MODEL), prm["g_pre_ffn1"], prm["g_post_ffn1"], prm["g_pre_mix"], *prm["ffn1"], t)
    (qkv,) = _proj_call(hn, prm["w_qkv"], ((0, OFF_Z),), (F32,), t, "proj_qkv_s")
    z, xbc, dt = _proj_call(hn, prm["w_zxd"], ZXD_SPLITS, (BF16, F32, F32), t, "proj_zxd_s")

    qkv5 = qkv.reshape(b, l, N_A_GROUPS, 3, A_OUT)
    new_caches, oes = [], []
    for g in range(N_A_GROUPS):
        cache = caches[g]
        newbuf, o, lse = _attn_sample_call(cache.reshape(b, cache.shape[1], 2 * A_OUT),
                                           qkv5[:, :, g, 0], qkv5[:, :, g, 1], qkv5[:, :, g, 2], g)
        new_caches.append(newbuf.reshape((1, b, newbuf.shape[1], 2, A_HEADS, A_HEAD_DIM)))
        oes.append(jnp.concatenate([o, lse], axis=-1).reshape(t, OE_COLS))

    pad = lambda a: jnp.pad(a.reshape(b, l, a.shape[-1]), ((0, 0), (0, B_CHUNK - l), (0, 0)))
    yb, conv_out, ssm_out = _ssd_call(pad(xbc), pad(z), pad(dt), conv_state, ssm_state, prm, l)
    h2 = _merge_call(oes, yb[:, :l].reshape(t, B_D_INNER), hn, h1, prm, t)
    (y,) = _ffn_call(h2, prm["g_pre_ffn2"], prm["g_post_ffn2"], None, *prm["ffn2"], t)
    return y.reshape(b, l, D_MODEL), new_caches, conv_out[None], ssm_out[None]


PROMPT_ROW_TILE = 512
PROMPT_QUERY_TILE = 256


def kernel(x_prompt, x_sample, cache_kv_w128, cache_kv_w512, cache_kv_w2048, state_conv, state_ssm, w_in, conv_w, conv_b, dt_bias, a_log, d_skip, ssd_norm_w, w_branch_a, w_branch_b, w_out, ffn1_gate, ffn1_up, ffn1_down, ffn2_gate, ffn2_up, ffn2_down, g_pre_ffn1, g_post_ffn1, g_pre_mix, g_post_mix, g_pre_ffn2, g_post_ffn2):
    assert w_in.shape[0] == 1, "single-layer trunk"
    prm = _prepare(*(p[0] for p in (
        w_in, conv_w, conv_b, dt_bias, a_log, d_skip, ssd_norm_w, w_branch_a, w_branch_b, w_out,
        ffn1_gate, ffn1_up, ffn1_down, ffn2_gate, ffn2_up, ffn2_down,
        g_pre_ffn1, g_post_ffn1, g_pre_mix, g_post_mix, g_pre_ffn2, g_post_ffn2)))
    y_p, kv_p, conv_p, ssm_p = _prompt_layer(x_prompt, prm, PROMPT_ROW_TILE, PROMPT_QUERY_TILE)
    y_s, kv_s, conv_s, ssm_s = _sample_layer(
        x_sample, (cache_kv_w128[0], cache_kv_w512[0], cache_kv_w2048[0]), state_conv[0], state_ssm[0], prm)
    return (y_p, y_s, kv_p[0], kv_p[1], kv_p[2], conv_p, ssm_p, kv_s[0], kv_s[1], kv_s[2], conv_s, ssm_s)
```

```python
import functools

import jax
import jax.numpy as jnp
import numpy as np
from jax import lax
from jax.experimental import pallas as pl
from jax.experimental.pallas import tpu as pltpu

F32 = jnp.float32
BF16 = jnp.bfloat16

D_MODEL = 1024
A_GROUPS = ((128, 1), (512, 4), (2048, 16))
N_A_GROUPS = len(A_GROUPS)
A_HEADS = 8
A_HEAD_DIM = 64
A_OUT = A_HEADS * A_HEAD_DIM
A_BAND = 128
B_D_INNER = 1536
B_HEAD_DIM = 64
B_HEADS = B_D_INNER // B_HEAD_DIM
B_GROUPS = 4
B_GROUP_HEADS = B_HEADS // B_GROUPS
B_GROUP_DIM = B_D_INNER // B_GROUPS
B_D_STATE = 128
B_CONV = 4
B_CHUNK = 128
B_CONV_DIM = B_D_INNER + 2 * B_GROUPS * B_D_STATE
D_FF = 2816
RMS_EPS = 1e-6

OFF_Z = N_A_GROUPS * 3 * A_OUT
OFF_XBC = OFF_Z + B_D_INNER
OFF_DT = OFF_XBC + B_CONV_DIM
OFF_GATE = OFF_DT + B_HEADS

LANES = 128
DT_PAD = LANES
OE_COLS = A_OUT + LANES
FF_CHUNK = D_FF // 2
VMEM_LIMIT = 56 * 1024 * 1024
NEG_INF = float("-inf")


def _const_spec(shape):
    nd = len(shape)
    return pl.BlockSpec(shape, lambda *_: (0,) * nd, pipeline_mode=pl.Buffered(1))


def _params(sem):
    return pltpu.CompilerParams(dimension_semantics=sem, vmem_limit_bytes=VMEM_LIMIT)


def _rms(x, g):
    return x * lax.rsqrt(jnp.mean(x * x, axis=-1, keepdims=True) + RMS_EPS) * g


def _silu(x):
    return x * jax.nn.sigmoid(x)


def _dot(a, b):
    return jnp.dot(a, b, preferred_element_type=F32)


def _dot_nt(a, b):
    return lax.dot_general(a, b, (((1,), (1,)), ((), ())), preferred_element_type=F32)


def _dot_tn(a, b):
    return lax.dot_general(a, b, (((0,), (0,)), ((), ())), preferred_element_type=F32)


def _split_bf16(a, parts):
    out = []
    r = a
    for i in range(parts):
        p = r.astype(BF16)
        out.append(p)
        if i + 1 < parts:
            r = r - p.astype(F32)
    return out


def _expand(a, e_ref):
    e = e_ref[...]
    hi, lo = _split_bf16(a, 2)
    return _dot(hi, e) + _dot(lo, e)


def _ffn_kernel(x_ref, gpre_ref, gpost_ref, gnext_ref, wg_ref, wu_ref, wd_ref, *rest, emit_next):
    if emit_next:
        h_ref, hn_ref, a_scr = rest
    else:
        h_ref, a_scr = rest
    x = x_ref[...]
    xn = _rms(x, gpre_ref[...]).astype(BF16)
    for c in range(D_FF // FF_CHUNK):
        sl = slice(c * FF_CHUNK, (c + 1) * FF_CHUNK)
        g = _dot(xn, wg_ref[:, sl])
        u = _dot(xn, wu_ref[:, sl])
        a_scr[:, sl] = (_silu(g) * u).astype(BF16)
    f = _dot(a_scr[...], wd_ref[...])
    h = x + 0.5 * _rms(f, gpost_ref[...])
    h_ref[...] = h
    if emit_next:
        hn_ref[...] = _rms(h, gnext_ref[...]).astype(BF16)


def _ffn_call(x, g_pre, g_post, g_next, wg, wu, wd, tm):
    t = x.shape[0]
    emit_next = g_next is not None
    row = lambda i: (i, 0)
    out_shape = [jax.ShapeDtypeStruct((t, D_MODEL), F32)]
    out_specs = [pl.BlockSpec((tm, D_MODEL), row)]
    if emit_next:
        out_shape.append(jax.ShapeDtypeStruct((t, D_MODEL), BF16))
        out_specs.append(pl.BlockSpec((tm, D_MODEL), row))
    return pl.pallas_call(
        functools.partial(_ffn_kernel, emit_next=emit_next),
        grid=(t // tm,),
        in_specs=[pl.BlockSpec((tm, D_MODEL), row),
                  _const_spec((1, D_MODEL)), _const_spec((1, D_MODEL)), _const_spec((1, D_MODEL)),
                  _const_spec((D_MODEL, D_FF)), _const_spec((D_MODEL, D_FF)), _const_spec((D_FF, D_MODEL))],
        out_specs=out_specs,
        out_shape=out_shape,
        scratch_shapes=[pltpu.VMEM((tm, D_FF), BF16)],
        compiler_params=_params(("parallel",)),
        name="ffn_next" if emit_next else "ffn",
    )(x, g_pre, g_post, g_next if emit_next else g_post, wg, wu, wd)


PROJ_CHUNK = 1536


def _proj_kernel(x_ref, w_ref, *o_refs, splits):
    x = x_ref[...]
    for o_ref, (a, b) in zip(o_refs, splits):
        for c0 in range(a, b, PROJ_CHUNK):
            c1 = min(b, c0 + PROJ_CHUNK)
            o_ref[:, c0 - a:c1 - a] = _dot(x, w_ref[:, c0:c1]).astype(o_ref.dtype)


def _proj_call(x, w, splits, dtypes, tm, name, rows=None, row_block=None):
    t, k = x.shape
    rows = t if rows is None else rows
    row_block = (lambda i: i) if row_block is None else row_block
    return pl.pallas_call(
        functools.partial(_proj_kernel, splits=splits),
        grid=(rows // tm,),
        in_specs=[pl.BlockSpec((tm, k), lambda i: (row_block(i), 0)), _const_spec(w.shape)],
        out_specs=[pl.BlockSpec((tm, b - a), lambda i: (i, 0)) for a, b in splits],
        out_shape=[jax.ShapeDtypeStruct((rows, b - a), dt) for (a, b), dt in zip(splits, dtypes)],
        compiler_params=_params(("parallel",)),
        name=name,
    )(x, w)


def _proj_t_kernel(x_ref, wt_ref, o_ref):
    o_ref[0] = _dot_nt(wt_ref[...], x_ref[...])


def _proj_t_call(x, wt, b, s, rows, tm, name):
    n, k = wt.shape
    per_b, first, sb = rows // tm, (s - rows) // tm, s // tm
    return pl.pallas_call(
        _proj_t_kernel,
        grid=(b, per_b),
        in_specs=[pl.BlockSpec((tm, k), lambda bi, j: (bi * sb + first + j, 0)), _const_spec(wt.shape)],
        out_specs=pl.BlockSpec((1, n, tm), lambda bi, j: (bi, 0, j)),
        out_shape=jax.ShapeDtypeStruct((b, n, rows), F32),
        compiler_params=_params(("parallel", "parallel")),
        name=name,
    )(x, wt)


QKV_COLS = 3 * A_OUT
SLABS = A_OUT // LANES


def _proj_qkv_kernel(x_ref, w_ref, *refs):
    o_refs, scrs = refs[:N_A_GROUPS], refs[N_A_GROUPS:]
    x = x_ref[...]
    tm = x.shape[0]
    for g, (o_ref, (_, dil)) in enumerate(zip(o_refs, A_GROUPS)):
        for part in range(3):
            c0 = part * A_OUT
            res = _dot(x, w_ref[:, g * QKV_COLS + c0:g * QKV_COLS + c0 + A_OUT])
            if dil == 1:
                o_ref[0, 0, :, c0:c0 + A_OUT] = res.astype(BF16)
                continue
            scr = scrs[g - 1]
            for j in range(SLABS):
                scr[part * SLABS + j] = res[:, j * LANES:(j + 1) * LANES]
            for r in range(dil):
                for j in range(SLABS):
                    rows = scr[part * SLABS + j, pl.ds(r, tm // dil, stride=dil), :]
                    o_ref[0, r, :, c0 + j * LANES:c0 + (j + 1) * LANES] = rows.astype(BF16)


def _proj_qkv_call(x, w, b, s, tm):
    t, k = x.shape
    tpb = s // tm
    dils = [dil for _, dil in A_GROUPS]
    return pl.pallas_call(
        _proj_qkv_kernel,
        grid=(t // tm,),
        in_specs=[pl.BlockSpec((tm, k), lambda i: (i, 0)), _const_spec(w.shape)],
        out_specs=[pl.BlockSpec((1, dil, tm // dil, QKV_COLS), lambda i: (i // tpb, 0, i % tpb, 0)) for dil in dils],
        out_shape=[jax.ShapeDtypeStruct((b, dil, s // dil, QKV_COLS), BF16) for dil in dils],
        scratch_shapes=[pltpu.VMEM((3 * SLABS, tm, LANES), F32) for dil in dils if dil > 1],
        compiler_params=_params(("parallel",)),
        name="proj_qkv",
    )(x, w)


def _attn_prompt_kernel(q_ref, kp_ref, kc_ref, vp_ref, vc_ref, o_ref, k_scr, v_scr, *, tq):
    n = pl.program_id(2)
    k_scr[0:A_BAND, :] = kp_ref[0, 0]
    k_scr[A_BAND:, :] = kc_ref[0, 0]
    v_scr[0:A_BAND, :] = vp_ref[0, 0]
    v_scr[A_BAND:, :] = vc_ref[0, 0]
    qi = lax.broadcasted_iota(jnp.int32, (A_BAND, 2 * A_BAND), 0)
    kj = lax.broadcasted_iota(jnp.int32, (A_BAND, 2 * A_BAND), 1)
    dist = qi + A_BAND - kj
    band = (dist >= 0) & (dist <= A_BAND)
    own = kj >= A_BAND
    lane = lax.broadcasted_iota(jnp.int32, (A_BAND, LANES), 1)
    low = lane < A_HEAD_DIM
    lane_row = lax.broadcasted_iota(jnp.int32, (1, LANES), 1)
    head_mask = ((lane_row < A_HEAD_DIM).astype(BF16), (lane_row >= A_HEAD_DIM).astype(BF16))

    def block(i, carry):
        r0 = pl.multiple_of(i * A_BAND, A_BAND)
        valid = band & (own | (n > 0) | (i > 0))
        lse_blk = jnp.zeros((A_BAND, LANES), F32)
        for j in range(A_OUT // LANES):
            cs = slice(j * LANES, (j + 1) * LANES)
            q2 = q_ref[0, 0, pl.ds(r0, A_BAND), cs]
            k2 = k_scr[pl.ds(r0, 2 * A_BAND), cs]
            v2 = v_scr[pl.ds(r0, 2 * A_BAND), cs]
            outs = []
            for half in range(2):
                s = _dot_nt(q2 * head_mask[half], k2) * (A_HEAD_DIM ** -0.5)
                s = jnp.where(valid, s, NEG_INF)
                m = jnp.max(s, axis=-1, keepdims=True)
                p = jnp.exp(s - m)
                den = jnp.sum(p, axis=-1, keepdims=True)
                outs.append(_dot(p.astype(BF16), v2) / den)
                lse_blk = jnp.where(lane == 2 * j + half, m + jnp.log(den), lse_blk)
            o_ref[0, 0, pl.ds(r0, A_BAND), cs] = jnp.where(low, outs[0], outs[1])
        o_ref[0, 0, pl.ds(r0, A_BAND), A_OUT:] = lse_blk
        return carry

    lax.fori_loop(0, tq // A_BAND, block, 0)


def _attn_prompt_call(qkv, tq, name):
    b, dil, ls, _ = qkv.shape
    tq = min(tq, ls)
    per = tq // A_BAND

    def cur(j):
        return lambda bi, r, n: (bi, r, n, j)

    def prev(j):
        return lambda bi, r, n: (bi, r, jnp.maximum(n * per - 1, 0), j)

    return pl.pallas_call(
        functools.partial(_attn_prompt_kernel, tq=tq),
        grid=(b, dil, ls // tq),
        in_specs=[pl.BlockSpec((1, 1, tq, A_OUT), cur(0)),
                  pl.BlockSpec((1, 1, A_BAND, A_OUT), prev(1)), pl.BlockSpec((1, 1, tq, A_OUT), cur(1)),
                  pl.BlockSpec((1, 1, A_BAND, A_OUT), prev(2)), pl.BlockSpec((1, 1, tq, A_OUT), cur(2))],
        out_specs=pl.BlockSpec((1, 1, tq, OE_COLS), lambda bi, r, n: (bi, r, n, 0)),
        out_shape=jax.ShapeDtypeStruct((b, dil, ls, OE_COLS), F32),
        scratch_shapes=[pltpu.VMEM((tq + A_BAND, A_OUT), BF16), pltpu.VMEM((tq + A_BAND, A_OUT), BF16)],
        compiler_params=_params(("parallel", "parallel", "arbitrary")),
        name=name,
    )(qkv, qkv, qkv, qkv, qkv)


S_ROWS = 128
NEW_ROWS = 16


def _attn_sample_kernel(ct_ref, kvnew_ref, q_ref, newct_ref, o_ref, lse_ref, *, window, dil, n_new):
    w = window
    ct = ct_ref[0]
    new = kvnew_ref[0]
    new_t = jnp.concatenate([new, jnp.zeros((LANES - NEW_ROWS, 2 * A_OUT), F32)], axis=0).T
    newct_ref[0] = jnp.concatenate([ct[:, n_new:], new_t[:, :n_new]], axis=1)

    r = lax.broadcasted_iota(jnp.int32, (S_ROWS, A_OUT), 0)
    c = lax.broadcasted_iota(jnp.int32, (S_ROWS, A_OUT), 1)
    head_lanes = (r // 8) == (c // A_HEAD_DIM)
    qm = jnp.where(head_lanes, q_ref[0], 0.0).astype(BF16)
    new_b = new.astype(BF16)
    scale = A_HEAD_DIM ** -0.5
    s_c = _dot(qm, ct[:A_OUT].astype(BF16)) * scale
    s_n = _dot_nt(qm, new_b[:, :A_OUT]) * scale

    def mask(s, first):
        rr = lax.broadcasted_iota(jnp.int32, s.shape, 0)
        pos = lax.broadcasted_iota(jnp.int32, s.shape, 1) + first
        d = w + (rr & (n_new - 1)) - pos
        return jnp.where((d >= 0) & (d <= window) & ((d & (dil - 1)) == 0), s, NEG_INF)

    s_c = mask(s_c, 0)
    s_n = mask(s_n, w)
    m = jnp.maximum(jnp.max(s_c, axis=-1, keepdims=True), jnp.max(s_n, axis=-1, keepdims=True))
    p_c = jnp.exp(s_c - m)
    p_n = jnp.exp(s_n - m)
    den = jnp.sum(p_c, axis=-1, keepdims=True) + jnp.sum(p_n, axis=-1, keepdims=True)
    pv = _dot_nt(p_c.astype(BF16), ct[A_OUT:].astype(BF16)) + _dot(p_n.astype(BF16), new_b[:, A_OUT:])
    o = jnp.where(head_lanes, pv / den, 0.0)
    lse = m + jnp.log(den)
    r2 = lax.broadcasted_iota(jnp.int32, (S_ROWS, LANES), 0)
    c2 = lax.broadcasted_iota(jnp.int32, (S_ROWS, LANES), 1)
    lse_sel = jnp.where((r2 // 8) == c2, lse, 0.0)
    o_acc = o[0:8]
    lse_acc = lse_sel[0:8]
    for h in range(1, A_HEADS):
        o_acc = o_acc + o[8 * h:8 * h + 8]
        lse_acc = lse_acc + lse_sel[8 * h:8 * h + 8]
    o_ref[0] = o_acc
    lse_ref[0] = lse_acc


def _attn_sample_call(cache, q, k, v, group):
    b, lb = cache.shape[:2]
    window, dil = A_GROUPS[group]
    n_new = q.shape[1]
    assert lb == window and n_new == 4
    ct = jnp.transpose(cache, (0, 2, 3, 4, 1)).reshape(b, 2 * A_OUT, lb)
    kvnew = jnp.pad(jnp.concatenate([k, v], axis=-1), ((0, 0), (0, NEW_ROWS - n_new), (0, 0)))
    q8 = jnp.pad(q, ((0, 0), (0, 8 - n_new), (0, 0)))
    q_rows = jnp.pad(jnp.tile(q8, (1, A_HEADS, 1)), ((0, 0), (0, S_ROWS - 8 * A_HEADS), (0, 0)))
    batch = lambda i: (i, 0, 0)
    newct, o, lse = pl.pallas_call(
        functools.partial(_attn_sample_kernel, window=window, dil=dil, n_new=n_new),
        grid=(b,),
        in_specs=[pl.BlockSpec((1, 2 * A_OUT, lb), batch),
                  pl.BlockSpec((1, NEW_ROWS, 2 * A_OUT), batch),
                  pl.BlockSpec((1, S_ROWS, A_OUT), batch)],
        out_specs=[pl.BlockSpec((1, 2 * A_OUT, window), batch),
                   pl.BlockSpec((1, 8, A_OUT), batch),
                   pl.BlockSpec((1, 8, LANES), batch)],
        out_shape=[jax.ShapeDtypeStruct((b, 2 * A_OUT, window), F32),
                   jax.ShapeDtypeStruct((b, 8, A_OUT), F32),
                   jax.ShapeDtypeStruct((b, 8, LANES), F32)],
        compiler_params=_params(("parallel",)),
        name=f"attn_sample_g{group}",
    )(ct, kvnew, q_rows)
    newbuf = jnp.transpose(newct.reshape(b, 2, A_HEADS, A_HEAD_DIM, window), (0, 4, 1, 2, 3))
    return newbuf, o[:, :n_new], lse[:, :n_new]


XP_OFF = 8


def _ssd_kernel(xbc_ref, z_ref, dt_ref, cinit_ref, sinit_ref, convw_ref, convb_ref, dtb_ref, alog_ref,
                dskip_ref, normw_ref, e_ref, yb_ref, convout_ref, ssmout_ref, state_scr, xp_scr, y_scr, *, valid, nc):
    t = B_CHUNK
    c = pl.program_id(1)

    @pl.when(c == 0)
    def _():
        state_scr[...] = sinit_ref[0].reshape(B_D_INNER, B_D_STATE)
        xp_scr[XP_OFF - (B_CONV - 1):XP_OFF, :] = cinit_ref[0]

    xp_scr[XP_OFF:XP_OFF + t, :] = xbc_ref[0]
    conv = convb_ref[...]
    for tap in range(B_CONV):
        lo = XP_OFF - (B_CONV - 1) + tap
        conv = conv + xp_scr[lo:lo + t, :] * convw_ref[tap:tap + 1, :]
    tail = xp_scr[XP_OFF + valid - (B_CONV - 1):XP_OFF + valid, :]
    xp_scr[XP_OFF - (B_CONV - 1):XP_OFF, :] = tail

    xc = _silu(conv)
    xs = xc[:, :B_D_INNER]
    row = lax.broadcasted_iota(jnp.int32, (t, t), 0)
    col = lax.broadcasted_iota(jnp.int32, (t, t), 1)
    causal = row >= col
    x_dt = dt_ref[0] + dtb_ref[...]
    dt = jnp.maximum(x_dt, 0.0) + jnp.log1p(jnp.exp(-jnp.abs(x_dt)))
    if valid < t:
        dt = jnp.where(row < valid, dt, 0.0)
    da = dt * (-jnp.exp(alog_ref[...]))
    tri = jnp.where(causal, 1.0, 0.0).astype(BF16)
    cs = sum(_dot(tri, part) for part in _split_bf16(da, 3))
    cs_t = cs.T
    dt_t = dt.T
    cs_last = cs[t - 1:t, :]
    both = jnp.concatenate([jnp.exp(cs), dt * jnp.exp(cs_last - cs)], axis=0)
    both_x = _expand(both, e_ref)
    ecs_x = both_x[:t]
    xdd = (xs * both_x[t:]).astype(BF16)
    xs_b = xs.astype(BF16)
    dec_t = jnp.exp(cs_t[:, t - 1:t])

    for g in range(B_GROUPS):
        bm_g = xc[:, B_D_INNER + g * B_D_STATE:B_D_INNER + (g + 1) * B_D_STATE].astype(BF16)
        cm_lo = B_D_INNER + (B_GROUPS + g) * B_D_STATE
        cm_g = xc[:, cm_lo:cm_lo + B_D_STATE].astype(BF16)
        cb = _dot_nt(cm_g, bm_g)
        gs = slice(g * B_GROUP_DIM, (g + 1) * B_GROUP_DIM)
        s_g = state_scr[gs, :]
        y_off = _dot_nt(cm_g, s_g.astype(BF16)) * ecs_x[:, gs]
        decs = []
        for r in range(B_GROUP_HEADS):
            h = g * B_GROUP_HEADS + r
            hs = slice(h * B_HEAD_DIM, (h + 1) * B_HEAD_DIM)
            seg = cs[:, h:h + 1] - cs_t[h:h + 1, :]
            lmat = jnp.exp(jnp.where(causal, seg, NEG_INF))
            mat = (cb * lmat * dt_t[h:h + 1, :]).astype(BF16)
            y_scr[:, hs] = _dot(mat, xs_b[:, hs]) + y_off[:, r * B_HEAD_DIM:(r + 1) * B_HEAD_DIM]
            decs.append(jnp.broadcast_to(dec_t[h:h + 1, :], (B_HEAD_DIM, B_D_STATE)))
        state_scr[gs, :] = s_g * jnp.concatenate(decs, axis=0) + _dot_tn(xdd[:, gs], bm_g)

    y = y_scr[...] + xs * dskip_ref[...]
    y = y * _silu(z_ref[0].astype(F32))
    for g in range(B_GROUPS):
        gs = slice(g * B_GROUP_DIM, (g + 1) * B_GROUP_DIM)
        yg = y[:, gs]
        yg = yg * lax.rsqrt(jnp.mean(yg * yg, axis=-1, keepdims=True) + RMS_EPS)
        yb_ref[0, :, gs] = (yg * normw_ref[:, gs]).astype(BF16)

    @pl.when(c == nc - 1)
    def _():
        convout_ref[0] = tail
        ssmout_ref[0] = state_scr[...].reshape(B_HEADS, B_HEAD_DIM, B_D_STATE)


def _ssd_call(xbc, z, dt, conv_init, ssm_init, prm, valid):
    b, l, _ = xbc.shape
    nc = l // B_CHUNK
    assert nc == 1 or valid == B_CHUNK
    tok = lambda bi, c: (bi, c, 0)
    per_b3 = lambda bi, c: (bi, 0, 0)
    per_b4 = lambda bi, c: (bi, 0, 0, 0)
    return pl.pallas_call(
        functools.partial(_ssd_kernel, valid=valid, nc=nc),
        grid=(b, nc),
        in_specs=[pl.BlockSpec((1, B_CHUNK, B_CONV_DIM), tok),
                  pl.BlockSpec((1, B_CHUNK, B_D_INNER), tok),
                  pl.BlockSpec((1, B_CHUNK, DT_PAD), tok),
                  pl.BlockSpec((1, B_CONV - 1, B_CONV_DIM), per_b3),
                  pl.BlockSpec((1, B_HEADS, B_HEAD_DIM, B_D_STATE), per_b4),
                  _const_spec((B_CONV, B_CONV_DIM)), _const_spec((1, B_CONV_DIM)),
                  _const_spec((1, DT_PAD)), _const_spec((1, DT_PAD)),
                  _const_spec((1, B_D_INNER)), _const_spec((1, B_D_INNER)),
                  _const_spec((LANES, B_D_INNER))],
        out_specs=[pl.BlockSpec((1, B_CHUNK, B_D_INNER), tok),
                   pl.BlockSpec((1, B_CONV - 1, B_CONV_DIM), per_b3),
                   pl.BlockSpec((1, B_HEADS, B_HEAD_DIM, B_D_STATE), per_b4)],
        out_shape=[jax.ShapeDtypeStruct((b, l, B_D_INNER), BF16),
                   jax.ShapeDtypeStruct((b, B_CONV - 1, B_CONV_DIM), F32),
                   jax.ShapeDtypeStruct((b, B_HEADS, B_HEAD_DIM, B_D_STATE), F32)],
        scratch_shapes=[pltpu.VMEM((B_D_INNER, B_D_STATE), F32),
                        pltpu.VMEM((XP_OFF + B_CHUNK, B_CONV_DIM), F32),
                        pltpu.VMEM((B_CHUNK, B_D_INNER), F32)],
        compiler_params=_params(("arbitrary", "arbitrary")),
        name="ssd",
    )(xbc, z, dt, conv_init, ssm_init, prm["conv_w"], prm["conv_b"], prm["dt_bias"], prm["a_log"],
      prm["d_skip"], prm["ssd_norm_w"], prm["expand"])


OE_SLABS = OE_COLS // LANES


def _merge_kernel(oe0_ref, oe1_ref, oe2_ref, yb_ref, hn_ref, h_ref, wgate_ref, wa_ref, wb_ref, wout_ref,
                  gpost_ref, e_ref, h2_ref, *scrs, dils):
    tm = h_ref.shape[0]
    outs, lses = [], []
    scrs = list(scrs)
    for oe, dil in zip((oe0_ref, oe1_ref, oe2_ref), dils):
        if dil == 1:
            outs.append(oe[:, :A_OUT])
            lses.append(oe[:, A_OUT:])
            continue
        scr = scrs.pop(0)
        for r in range(dil):
            for j in range(OE_SLABS):
                scr[j, pl.ds(r, tm // dil, stride=dil), :] = oe[0, r, :, j * LANES:(j + 1) * LANES]
        outs.append(jnp.concatenate([scr[j] for j in range(OE_SLABS - 1)], axis=1))
        lses.append(scr[OE_SLABS - 1])
    m = jnp.maximum(jnp.maximum(lses[0], lses[1]), lses[2])
    es = [jnp.exp(l - m) for l in lses]
    tot = es[0] + es[1] + es[2]
    o_a = None
    for o, e in zip(outs, es):
        term = _expand(e / tot, e_ref) * o
        o_a = term if o_a is None else o_a + term
    pa = _dot(o_a.astype(BF16), wa_ref[...])
    pb = _dot(yb_ref[...], wb_ref[...])
    gates = jax.nn.sigmoid(_dot(hn_ref[...], wgate_ref[...]))
    merged = gates[:, :D_MODEL] * pa + gates[:, D_MODEL:] * pb
    mix = _dot(merged.astype(BF16), wout_ref[...])
    h2_ref[...] = h_ref[...] + _rms(mix, gpost_ref[...])


def _merge_call(oes, yb, hn, h, prm, tm):
    t = h.shape[0]
    row = lambda i: (i, 0)
    dils, oe_specs = [], []
    for oe in oes:
        if oe.ndim == 2:
            dils.append(1)
            oe_specs.append(pl.BlockSpec((tm, OE_COLS), row))
        else:
            dil, tpb = oe.shape[1], oe.shape[1] * oe.shape[2] // tm
            dils.append(dil)
            oe_specs.append(pl.BlockSpec((1, dil, tm // dil, OE_COLS),
                                         lambda i, tpb=tpb: (i // tpb, 0, i % tpb, 0)))
    return pl.pallas_call(
        functools.partial(_merge_kernel, dils=tuple(dils)),
        grid=(t // tm,),
        scratch_shapes=[pltpu.VMEM((OE_SLABS, tm, LANES), F32) for dil in dils if dil > 1],
        in_specs=oe_specs + [
            pl.BlockSpec((tm, B_D_INNER), row), pl.BlockSpec((tm, D_MODEL), row), pl.BlockSpec((tm, D_MODEL), row),
            _const_spec((D_MODEL, 2 * D_MODEL)), _const_spec((A_OUT, D_MODEL)),
            _const_spec((B_D_INNER, D_MODEL)), _const_spec((D_MODEL, D_MODEL)),
            _const_spec((1, D_MODEL)), _const_spec((LANES, A_OUT))],
        out_specs=pl.BlockSpec((tm, D_MODEL), row),
        out_shape=jax.ShapeDtypeStruct((t, D_MODEL), F32),
        compiler_params=_params(("parallel",)),
        name="merge",
    )(*oes, yb, hn, h, prm["w_gate"], prm["w_branch_a"], prm["w_branch_b"], prm["w_out"],
      prm["g_post_mix"], prm["expand8"])


def _expand_matrix():
    e = np.zeros((LANES, B_D_INNER), np.float32)
    for h in range(B_HEADS):
        e[h, h * B_HEAD_DIM:(h + 1) * B_HEAD_DIM] = 1.0
    return e


def _prepare(w_in, conv_w, conv_b, dt_bias, a_log, d_skip, ssd_norm_w, w_branch_a, w_branch_b, w_out,
             ffn1_gate, ffn1_up, ffn1_down, ffn2_gate, ffn2_up, ffn2_down,
             g_pre_ffn1, g_post_ffn1, g_pre_mix, g_post_mix, g_pre_ffn2, g_post_ffn2):
    lane_pad = lambda v: jnp.pad(v, (0, DT_PAD - v.shape[0]))[None, :]
    w_zxd = jnp.pad(w_in[:, OFF_Z:OFF_GATE], ((0, 0), (0, DT_PAD - B_HEADS)))
    e = _expand_matrix()
    return {
        "w_qkv": w_in[:, :OFF_Z].astype(BF16),
        "w_kv_t": [w_in[:, g * QKV_COLS + A_OUT:(g + 1) * QKV_COLS].T.astype(BF16) for g in range(N_A_GROUPS)],
        "w_zxd": w_zxd.astype(BF16),
        "w_gate": w_in[:, OFF_GATE:].astype(BF16),
        "conv_w": conv_w, "conv_b": conv_b[None, :],
        "dt_bias": lane_pad(dt_bias), "a_log": lane_pad(a_log),
        "d_skip": jnp.repeat(d_skip, B_HEAD_DIM)[None, :],
        "ssd_norm_w": ssd_norm_w[None, :],
        "expand": jnp.asarray(e, BF16), "expand8": jnp.asarray(e[:, :A_OUT], BF16),
        "w_branch_a": w_branch_a.astype(BF16), "w_branch_b": w_branch_b.astype(BF16), "w_out": w_out.astype(BF16),
        "ffn1": (ffn1_gate.astype(BF16), ffn1_up.astype(BF16), ffn1_down.astype(BF16)),
        "ffn2": (ffn2_gate.astype(BF16), ffn2_up.astype(BF16), ffn2_down.astype(BF16)),
        "g_pre_ffn1": g_pre_ffn1[None, :], "g_post_ffn1": g_post_ffn1[None, :],
        "g_pre_mix": g_pre_mix[None, :], "g_post_mix": g_post_mix[None, :],
        "g_pre_ffn2": g_pre_ffn2[None, :], "g_post_ffn2": g_post_ffn2[None, :],
    }


ZXD_SPLITS = ((0, B_D_INNER), (B_D_INNER, B_D_INNER + B_CONV_DIM), (B_D_INNER + B_CONV_DIM, B_D_INNER + B_CONV_DIM + DT_PAD))


def _prompt_layer(x, prm, tm, tq):
    b, s, _ = x.shape
    t = b * s
    h1, hn = _ffn_call(x.reshape(t, D_MODEL), prm["g_pre_ffn1"], prm["g_post_ffn1"], prm["g_pre_mix"], *prm["ffn1"], tm)
    qkvs = _proj_qkv_call(hn, prm["w_qkv"], b, s, tm)
    z, xbc, dt = _proj_call(hn, prm["w_zxd"], ZXD_SPLITS, (BF16, F32, F32), tm, "proj_zxd")

    kv_tails = []
    for g, (window, _) in enumerate(A_GROUPS):
        rows = min(window, s)
        kvt = _proj_t_call(hn, prm["w_kv_t"][g], b, s, rows, min(rows, tm), f"proj_kv_g{g}")
        kvt = kvt.reshape(b, 2, A_HEADS, A_HEAD_DIM, rows)
        kv_tails.append(jnp.transpose(kvt, (0, 4, 1, 2, 3))[None])

    oes = [_attn_prompt_call(qkvs[g], tq, f"attn_prompt_g{g}") for g in range(N_A_GROUPS)]
    oes[0] = oes[0].reshape(t, OE_COLS)
    yb, conv_out, ssm_out = _ssd_call(
        xbc.reshape(b, s, B_CONV_DIM), z.reshape(b, s, B_D_INNER), dt.reshape(b, s, DT_PAD),
        jnp.zeros((b, B_CONV - 1, B_CONV_DIM), F32), jnp.zeros((b, B_HEADS, B_HEAD_DIM, B_D_STATE), F32),
        prm, B_CHUNK)
    h2 = _merge_call(oes, yb.reshape(t, B_D_INNER), hn, h1, prm, tm)
    (y,) = _ffn_call(h2, prm["g_pre_ffn2"], prm["g_post_ffn2"], None, *prm["ffn2"], tm)
    return y.reshape(b, s, D_MODEL), kv_tails, conv_out[None], ssm_out[None]


def _sample_layer(x, caches, conv_state, ssm_state, prm):
    b, l, _ = x.shape
    t = b * l
    h1, hn = _ffn_call(x.reshape(t, D_MODEL), prm["g_pre_ffn1"], prm["g_post_ffn1"], prm["g_pre_mix"], *prm["ffn1"], t)
    (qkv,) = _proj_call(hn, prm["w_qkv"], ((0, OFF_Z),), (F32,), t, "proj_qkv_s")
    z, xbc, dt = _proj_call(hn, prm["w_zxd"], ZXD_SPLITS, (BF16, F32, F32), t, "proj_zxd_s")

    qkv5 = qkv.reshape(b, l, N_A_GROUPS, 3, A_OUT)
    new_caches, oes = [], []
    for g in range(N_A_GROUPS):
        newbuf, o, lse = _attn_sample_call(caches[g], qkv5[:, :, g, 0], qkv5[:, :, g, 1], qkv5[:, :, g, 2], g)
        new_caches.append(newbuf[None])
        oes.append(jnp.concatenate([o, lse], axis=-1).reshape(t, OE_COLS))

    pad = lambda a: jnp.pad(a.reshape(b, l, a.shape[-1]), ((0, 0), (0, B_CHUNK - l), (0, 0)))
    yb, conv_out, ssm_out = _ssd_call(pad(xbc), pad(z), pad(dt), conv_state, ssm_state, prm, l)
    h2 = _merge_call(oes, yb[:, :l].reshape(t, B_D_INNER), hn, h1, prm, t)
    (y,) = _ffn_call(h2, prm["g_pre_ffn2"], prm["g_post_ffn2"], None, *prm["ffn2"], t)
    return y.reshape(b, l, D_MODEL), new_caches, conv_out[None], ssm_out[None]


PROMPT_ROW_TILE = 512
PROMPT_QUERY_TILE = 512


def kernel(x_prompt, x_sample, cache_kv_w128, cache_kv_w512, cache_kv_w2048, state_conv, state_ssm, w_in, conv_w, conv_b, dt_bias, a_log, d_skip, ssd_norm_w, w_branch_a, w_branch_b, w_out, ffn1_gate, ffn1_up, ffn1_down, ffn2_gate, ffn2_up, ffn2_down, g_pre_ffn1, g_post_ffn1, g_pre_mix, g_post_mix, g_pre_ffn2, g_post_ffn2):
    assert w_in.shape[0] == 1, "single-layer trunk"
    prm = _prepare(*(p[0] for p in (
        w_in, conv_w, conv_b, dt_bias, a_log, d_skip, ssd_norm_w, w_branch_a, w_branch_b, w_out,
        ffn1_gate, ffn1_up, ffn1_down, ffn2_gate, ffn2_up, ffn2_down,
        g_pre_ffn1, g_post_ffn1, g_pre_mix, g_post_mix, g_pre_ffn2, g_post_ffn2)))
    y_p, kv_p, conv_p, ssm_p = _prompt_layer(x_prompt, prm, PROMPT_ROW_TILE, PROMPT_QUERY_TILE)
    y_s, kv_s, conv_s, ssm_s = _sample_layer(
        x_sample, (cache_kv_w128[0], cache_kv_w512[0], cache_kv_w2048[0]), state_conv[0], state_ssm[0], prm)
    return (y_p, y_s, kv_p[0], kv_p[1], kv_p[2], conv_p, ssm_p, kv_s[0], kv_s[1], kv_s[2], conv_s, ssm_s)
```

```python
import functools

import jax
import jax.numpy as jnp
import numpy as np
from jax import lax
from jax.experimental import pallas as pl
from jax.experimental.pallas import tpu as pltpu

F32 = jnp.float32
BF16 = jnp.bfloat16

D_MODEL = 1024
A_GROUPS = ((128, 1), (512, 4), (2048, 16))
N_A_GROUPS = len(A_GROUPS)
A_HEADS = 8
A_HEAD_DIM = 64
A_OUT = A_HEADS * A_HEAD_DIM
A_BAND = 128
B_D_INNER = 1536
B_HEAD_DIM = 64
B_HEADS = B_D_INNER // B_HEAD_DIM
B_GROUPS = 4
B_GROUP_HEADS = B_HEADS // B_GROUPS
B_GROUP_DIM = B_D_INNER // B_GROUPS
B_D_STATE = 128
B_CONV = 4
B_CHUNK = 128
B_CONV_DIM = B_D_INNER + 2 * B_GROUPS * B_D_STATE
D_FF = 2816
RMS_EPS = 1e-6

OFF_Z = N_A_GROUPS * 3 * A_OUT
OFF_XBC = OFF_Z + B_D_INNER
OFF_DT = OFF_XBC + B_CONV_DIM
OFF_GATE = OFF_DT + B_HEADS

LANES = 128
DT_PAD = LANES
OE_COLS = A_OUT + LANES
FF_CHUNK = 256
VMEM_LIMIT = 56 * 1024 * 1024
NEG_INF = float("-inf")


def _const_spec(shape):
    nd = len(shape)
    return pl.BlockSpec(shape, lambda *_: (0,) * nd, pipeline_mode=pl.Buffered(1))


def _params(sem):
    return pltpu.CompilerParams(dimension_semantics=sem, vmem_limit_bytes=VMEM_LIMIT)


def _rms(x, g):
    return x * lax.rsqrt(jnp.mean(x * x, axis=-1, keepdims=True) + RMS_EPS) * g


def _sigmoid(x):
    return 0.5 * jnp.tanh(0.5 * x) + 0.5


def _silu(x):
    return x * _sigmoid(x)


def _dot(a, b):
    return jnp.dot(a, b, preferred_element_type=F32)


def _dot_nt(a, b):
    return lax.dot_general(a, b, (((1,), (1,)), ((), ())), preferred_element_type=F32)


def _dot_tn(a, b):
    return lax.dot_general(a, b, (((0,), (0,)), ((), ())), preferred_element_type=F32)


def _split_bf16(a, parts):
    out = []
    r = a
    for i in range(parts):
        p = r.astype(BF16)
        out.append(p)
        if i + 1 < parts:
            r = r - p.astype(F32)
    return out


def _expand(a, e_ref):
    e = e_ref[...]
    hi, lo = _split_bf16(a, 2)
    return _dot(hi, e) + _dot(lo, e)


def _ffn_kernel(x_ref, gpre_ref, gpost_ref, gnext_ref, wg_ref, wu_ref, wd_ref, *rest, emit_next):
    if emit_next:
        h_ref, hn_ref, a_scr = rest
    else:
        h_ref, a_scr = rest
    x = x_ref[...]
    xn = _rms(x, gpre_ref[...]).astype(BF16)
    for c in range(D_FF // FF_CHUNK):
        sl = slice(c * FF_CHUNK, (c + 1) * FF_CHUNK)
        g = _dot(xn, wg_ref[:, sl])
        u = _dot(xn, wu_ref[:, sl])
        a_scr[:, sl] = (_silu(g) * u).astype(BF16)
    f = _dot(a_scr[...], wd_ref[...])
    h = x + 0.5 * _rms(f, gpost_ref[...])
    h_ref[...] = h
    if emit_next:
        hn_ref[...] = _rms(h, gnext_ref[...]).astype(BF16)


def _ffn_call(x, g_pre, g_post, g_next, wg, wu, wd, tm):
    t = x.shape[0]
    emit_next = g_next is not None
    row = lambda i: (i, 0)
    out_shape = [jax.ShapeDtypeStruct((t, D_MODEL), F32)]
    out_specs = [pl.BlockSpec((tm, D_MODEL), row)]
    if emit_next:
        out_shape.append(jax.ShapeDtypeStruct((t, D_MODEL), BF16))
        out_specs.append(pl.BlockSpec((tm, D_MODEL), row))
    return pl.pallas_call(
        functools.partial(_ffn_kernel, emit_next=emit_next),
        grid=(t // tm,),
        in_specs=[pl.BlockSpec((tm, D_MODEL), row),
                  _const_spec((1, D_MODEL)), _const_spec((1, D_MODEL)), _const_spec((1, D_MODEL)),
                  _const_spec((D_MODEL, D_FF)), _const_spec((D_MODEL, D_FF)), _const_spec((D_FF, D_MODEL))],
        out_specs=out_specs,
        out_shape=out_shape,
        scratch_shapes=[pltpu.VMEM((tm, D_FF), BF16)],
        compiler_params=_params(("parallel",)),
        name="ffn_next" if emit_next else "ffn",
    )(x, g_pre, g_post, g_next if emit_next else g_post, wg, wu, wd)


PROJ_CHUNK = 1536


def _proj_kernel(x_ref, w_ref, *o_refs, splits):
    x = x_ref[...]
    for o_ref, (a, b) in zip(o_refs, splits):
        for c0 in range(a, b, PROJ_CHUNK):
            c1 = min(b, c0 + PROJ_CHUNK)
            o_ref[:, c0 - a:c1 - a] = _dot(x, w_ref[:, c0:c1]).astype(o_ref.dtype)


def _proj_call(x, w, splits, dtypes, tm, name, rows=None, row_block=None):
    t, k = x.shape
    rows = t if rows is None else rows
    row_block = (lambda i: i) if row_block is None else row_block
    return pl.pallas_call(
        functools.partial(_proj_kernel, splits=splits),
        grid=(rows // tm,),
        in_specs=[pl.BlockSpec((tm, k), lambda i: (row_block(i), 0)), _const_spec(w.shape)],
        out_specs=[pl.BlockSpec((tm, b - a), lambda i: (i, 0)) for a, b in splits],
        out_shape=[jax.ShapeDtypeStruct((rows, b - a), dt) for (a, b), dt in zip(splits, dtypes)],
        compiler_params=_params(("parallel",)),
        name=name,
    )(x, w)


def _proj_t_kernel(x_ref, wt_ref, o_ref):
    o_ref[0] = _dot_nt(wt_ref[...], x_ref[...])


def _proj_t_call(x, wt, b, s, rows, tm, name):
    n, k = wt.shape
    per_b, first, sb = rows // tm, (s - rows) // tm, s // tm
    return pl.pallas_call(
        _proj_t_kernel,
        grid=(b, per_b),
        in_specs=[pl.BlockSpec((tm, k), lambda bi, j: (bi * sb + first + j, 0)), _const_spec(wt.shape)],
        out_specs=pl.BlockSpec((1, n, tm), lambda bi, j: (bi, 0, j)),
        out_shape=jax.ShapeDtypeStruct((b, n, rows), F32),
        compiler_params=_params(("parallel", "parallel")),
        name=name,
    )(x, wt)


QKV_COLS = 3 * A_OUT
SLABS = A_OUT // LANES


def _proj_qkv_kernel(x_ref, w_ref, *refs):
    o_refs, scrs = refs[:N_A_GROUPS], refs[N_A_GROUPS:]
    x = x_ref[...]
    tm = x.shape[0]
    for g, (o_ref, (_, dil)) in enumerate(zip(o_refs, A_GROUPS)):
        for part in range(3):
            c0 = part * A_OUT
            res = _dot(x, w_ref[:, g * QKV_COLS + c0:g * QKV_COLS + c0 + A_OUT])
            if dil == 1:
                o_ref[0, 0, :, c0:c0 + A_OUT] = res.astype(BF16)
                continue
            scr = scrs[g - 1]
            for j in range(SLABS):
                scr[part * SLABS + j] = res[:, j * LANES:(j + 1) * LANES]
            for r in range(dil):
                for j in range(SLABS):
                    rows = scr[part * SLABS + j, pl.ds(r, tm // dil, stride=dil), :]
                    o_ref[0, r, :, c0 + j * LANES:c0 + (j + 1) * LANES] = rows.astype(BF16)


def _proj_qkv_call(x, w, b, s, tm):
    t, k = x.shape
    tpb = s // tm
    dils = [dil for _, dil in A_GROUPS]
    return pl.pallas_call(
        _proj_qkv_kernel,
        grid=(t // tm,),
        in_specs=[pl.BlockSpec((tm, k), lambda i: (i, 0)), _const_spec(w.shape)],
        out_specs=[pl.BlockSpec((1, dil, tm // dil, QKV_COLS), lambda i: (i // tpb, 0, i % tpb, 0)) for dil in dils],
        out_shape=[jax.ShapeDtypeStruct((b, dil, s // dil, QKV_COLS), BF16) for dil in dils],
        scratch_shapes=[pltpu.VMEM((3 * SLABS, tm, LANES), F32) for dil in dils if dil > 1],
        compiler_params=_params(("parallel",)),
        name="proj_qkv",
    )(x, w)


def _attn_prompt_kernel(q_ref, kp_ref, kc_ref, vp_ref, vc_ref, o_ref, k_scr, v_scr, *, tq):
    n = pl.program_id(2)
    k_scr[0:A_BAND, :] = kp_ref[0, 0]
    k_scr[A_BAND:, :] = kc_ref[0, 0]
    v_scr[0:A_BAND, :] = vp_ref[0, 0]
    v_scr[A_BAND:, :] = vc_ref[0, 0]
    qi = lax.broadcasted_iota(jnp.int32, (A_BAND, 2 * A_BAND), 0)
    kj = lax.broadcasted_iota(jnp.int32, (A_BAND, 2 * A_BAND), 1)
    dist = qi + A_BAND - kj
    band = (dist >= 0) & (dist <= A_BAND)
    own = kj >= A_BAND
    lane = lax.broadcasted_iota(jnp.int32, (A_BAND, LANES), 1)
    low = lane < A_HEAD_DIM
    lane_row = lax.broadcasted_iota(jnp.int32, (1, LANES), 1)
    head_mask = ((lane_row < A_HEAD_DIM).astype(BF16), (lane_row >= A_HEAD_DIM).astype(BF16))

    def block(i, carry):
        r0 = pl.multiple_of(i * A_BAND, A_BAND)
        valid = band & (own | (n > 0) | (i > 0))
        lse_blk = jnp.zeros((A_BAND, LANES), F32)
        for j in range(A_OUT // LANES):
            cs = slice(j * LANES, (j + 1) * LANES)
            q2 = q_ref[0, 0, pl.ds(r0, A_BAND), cs]
            k2 = k_scr[pl.ds(r0, 2 * A_BAND), cs]
            v2 = v_scr[pl.ds(r0, 2 * A_BAND), cs]
            outs = []
            for half in range(2):
                s = _dot_nt(q2 * head_mask[half], k2) * (A_HEAD_DIM ** -0.5)
                s = jnp.where(valid, s, NEG_INF)
                m = jnp.max(s, axis=-1, keepdims=True)
                p = jnp.exp(s - m)
                den = jnp.sum(p, axis=-1, keepdims=True)
                outs.append(_dot(p.astype(BF16), v2) / den)
                lse_blk = jnp.where(lane == 2 * j + half, m + jnp.log(den), lse_blk)
            o_ref[0, 0, pl.ds(r0, A_BAND), cs] = jnp.where(low, outs[0], outs[1])
        o_ref[0, 0, pl.ds(r0, A_BAND), A_OUT:] = lse_blk
        return carry

    lax.fori_loop(0, tq // A_BAND, block, 0)


def _attn_prompt_call(qkv, tq, name):
    b, dil, ls, _ = qkv.shape
    tq = min(tq, ls)
    per = tq // A_BAND

    def cur(j):
        return lambda bi, r, n: (bi, r, n, j)

    def prev(j):
        return lambda bi, r, n: (bi, r, jnp.maximum(n * per - 1, 0), j)

    return pl.pallas_call(
        functools.partial(_attn_prompt_kernel, tq=tq),
        grid=(b, dil, ls // tq),
        in_specs=[pl.BlockSpec((1, 1, tq, A_OUT), cur(0)),
                  pl.BlockSpec((1, 1, A_BAND, A_OUT), prev(1)), pl.BlockSpec((1, 1, tq, A_OUT), cur(1)),
                  pl.BlockSpec((1, 1, A_BAND, A_OUT), prev(2)), pl.BlockSpec((1, 1, tq, A_OUT), cur(2))],
        out_specs=pl.BlockSpec((1, 1, tq, OE_COLS), lambda bi, r, n: (bi, r, n, 0)),
        out_shape=jax.ShapeDtypeStruct((b, dil, ls, OE_COLS), F32),
        scratch_shapes=[pltpu.VMEM((tq + A_BAND, A_OUT), BF16), pltpu.VMEM((tq + A_BAND, A_OUT), BF16)],
        compiler_params=_params(("parallel", "parallel", "arbitrary")),
        name=name,
    )(qkv, qkv, qkv, qkv, qkv)


S_ROWS = 128
NEW_ROWS = 16


def _attn_sample_kernel(ct_ref, kvnew_ref, q_ref, newct_ref, o_ref, lse_ref, *, window, dil, n_new):
    w = window
    ct = ct_ref[0]
    new = kvnew_ref[0]
    new_t = jnp.concatenate([new, jnp.zeros((LANES - NEW_ROWS, 2 * A_OUT), F32)], axis=0).T
    newct_ref[0] = jnp.concatenate([ct[:, n_new:], new_t[:, :n_new]], axis=1)

    r = lax.broadcasted_iota(jnp.int32, (S_ROWS, A_OUT), 0)
    c = lax.broadcasted_iota(jnp.int32, (S_ROWS, A_OUT), 1)
    head_lanes = (r // 8) == (c // A_HEAD_DIM)
    qm = jnp.where(head_lanes, q_ref[0], 0.0).astype(BF16)
    new_b = new.astype(BF16)
    scale = A_HEAD_DIM ** -0.5
    s_c = _dot(qm, ct[:A_OUT].astype(BF16)) * scale
    s_n = _dot_nt(qm, new_b[:, :A_OUT]) * scale

    def mask(s, first):
        rr = lax.broadcasted_iota(jnp.int32, s.shape, 0)
        pos = lax.broadcasted_iota(jnp.int32, s.shape, 1) + first
        d = w + (rr & (n_new - 1)) - pos
        return jnp.where((d >= 0) & (d <= window) & ((d & (dil - 1)) == 0), s, NEG_INF)

    s_c = mask(s_c, 0)
    s_n = mask(s_n, w)
    m = jnp.maximum(jnp.max(s_c, axis=-1, keepdims=True), jnp.max(s_n, axis=-1, keepdims=True))
    p_c = jnp.exp(s_c - m)
    p_n = jnp.exp(s_n - m)
    den = jnp.sum(p_c, axis=-1, keepdims=True) + jnp.sum(p_n, axis=-1, keepdims=True)
    pv = _dot_nt(p_c.astype(BF16), ct[A_OUT:].astype(BF16)) + _dot(p_n.astype(BF16), new_b[:, A_OUT:])
    o = jnp.where(head_lanes, pv / den, 0.0)
    lse = m + jnp.log(den)
    r2 = lax.broadcasted_iota(jnp.int32, (S_ROWS, LANES), 0)
    c2 = lax.broadcasted_iota(jnp.int32, (S_ROWS, LANES), 1)
    lse_sel = jnp.where((r2 // 8) == c2, lse, 0.0)
    o_acc = o[0:8]
    lse_acc = lse_sel[0:8]
    for h in range(1, A_HEADS):
        o_acc = o_acc + o[8 * h:8 * h + 8]
        lse_acc = lse_acc + lse_sel[8 * h:8 * h + 8]
    o_ref[0] = o_acc
    lse_ref[0] = lse_acc


def _attn_sample_call(cache, q, k, v, group):
    b, lb = cache.shape[:2]
    window, dil = A_GROUPS[group]
    n_new = q.shape[1]
    assert lb == window and n_new == 4
    ct = jnp.transpose(cache, (0, 2, 3, 4, 1)).reshape(b, 2 * A_OUT, lb)
    kvnew = jnp.pad(jnp.concatenate([k, v], axis=-1), ((0, 0), (0, NEW_ROWS - n_new), (0, 0)))
    q8 = jnp.pad(q, ((0, 0), (0, 8 - n_new), (0, 0)))
    q_rows = jnp.pad(jnp.tile(q8, (1, A_HEADS, 1)), ((0, 0), (0, S_ROWS - 8 * A_HEADS), (0, 0)))
    batch = lambda i: (i, 0, 0)
    newct, o, lse = pl.pallas_call(
        functools.partial(_attn_sample_kernel, window=window, dil=dil, n_new=n_new),
        grid=(b,),
        in_specs=[pl.BlockSpec((1, 2 * A_OUT, lb), batch),
                  pl.BlockSpec((1, NEW_ROWS, 2 * A_OUT), batch),
                  pl.BlockSpec((1, S_ROWS, A_OUT), batch)],
        out_specs=[pl.BlockSpec((1, 2 * A_OUT, window), batch),
                   pl.BlockSpec((1, 8, A_OUT), batch),
                   pl.BlockSpec((1, 8, LANES), batch)],
        out_shape=[jax.ShapeDtypeStruct((b, 2 * A_OUT, window), F32),
                   jax.ShapeDtypeStruct((b, 8, A_OUT), F32),
                   jax.ShapeDtypeStruct((b, 8, LANES), F32)],
        compiler_params=_params(("parallel",)),
        name=f"attn_sample_g{group}",
    )(ct, kvnew, q_rows)
    newbuf = jnp.transpose(newct.reshape(b, 2, A_HEADS, A_HEAD_DIM, window), (0, 4, 1, 2, 3))
    return newbuf, o[:, :n_new], lse[:, :n_new]


XP_OFF = 8


def _ssd_kernel(xbc_ref, z_ref, dt_ref, cinit_ref, sinit_ref, convw_ref, convb_ref, dtb_ref, alog_ref,
                dskip_ref, normw_ref, e_ref, yb_ref, convout_ref, ssmout_ref, state_scr, xp_scr, y_scr, *, valid, nc):
    t = B_CHUNK
    c = pl.program_id(1)

    @pl.when(c == 0)
    def _():
        state_scr[...] = sinit_ref[0].reshape(B_D_INNER, B_D_STATE)
        xp_scr[XP_OFF - (B_CONV - 1):XP_OFF, :] = cinit_ref[0]

    xp_scr[XP_OFF:XP_OFF + t, :] = xbc_ref[0]
    conv = convb_ref[...]
    for tap in range(B_CONV):
        lo = XP_OFF - (B_CONV - 1) + tap
        conv = conv + xp_scr[lo:lo + t, :] * convw_ref[tap:tap + 1, :]
    tail = xp_scr[XP_OFF + valid - (B_CONV - 1):XP_OFF + valid, :]
    xp_scr[XP_OFF - (B_CONV - 1):XP_OFF, :] = tail

    xc = _silu(conv)
    xs = xc[:, :B_D_INNER]
    row = lax.broadcasted_iota(jnp.int32, (t, t), 0)
    col = lax.broadcasted_iota(jnp.int32, (t, t), 1)
    causal = row >= col
    x_dt = dt_ref[0] + dtb_ref[...]
    dt = jnp.maximum(x_dt, 0.0) + jnp.log1p(jnp.exp(-jnp.abs(x_dt)))
    if valid < t:
        dt = jnp.where(row < valid, dt, 0.0)
    da = dt * (-jnp.exp(alog_ref[...]))
    tri = jnp.where(causal, 1.0, 0.0).astype(BF16)
    cs = sum(_dot(tri, part) for part in _split_bf16(da, 3))
    cs_t = cs.T
    dt_t = dt.T
    cs_last = cs[t - 1:t, :]
    both = jnp.concatenate([jnp.exp(cs), dt * jnp.exp(cs_last - cs)], axis=0)
    both_x = _expand(both, e_ref)
    ecs_x = both_x[:t]
    xdd = (xs * both_x[t:]).astype(BF16)
    xs_b = xs.astype(BF16)
    dec_t = jnp.exp(cs_t[:, t - 1:t])

    for g in range(B_GROUPS):
        bm_g = xc[:, B_D_INNER + g * B_D_STATE:B_D_INNER + (g + 1) * B_D_STATE].astype(BF16)
        cm_lo = B_D_INNER + (B_GROUPS + g) * B_D_STATE
        cm_g = xc[:, cm_lo:cm_lo + B_D_STATE].astype(BF16)
        cb = _dot_nt(cm_g, bm_g)
        gs = slice(g * B_GROUP_DIM, (g + 1) * B_GROUP_DIM)
        s_g = state_scr[gs, :]
        y_off = _dot_nt(cm_g, s_g.astype(BF16)) * ecs_x[:, gs]
        decs = []
        for r in range(B_GROUP_HEADS):
            h = g * B_GROUP_HEADS + r
            hs = slice(h * B_HEAD_DIM, (h + 1) * B_HEAD_DIM)
            seg = cs[:, h:h + 1] - cs_t[h:h + 1, :]
            lmat = jnp.exp(jnp.where(causal, seg, NEG_INF))
            mat = (cb * lmat * dt_t[h:h + 1, :]).astype(BF16)
            y_scr[:, hs] = _dot(mat, xs_b[:, hs]) + y_off[:, r * B_HEAD_DIM:(r + 1) * B_HEAD_DIM]
            decs.append(jnp.broadcast_to(dec_t[h:h + 1, :], (B_HEAD_DIM, B_D_STATE)))
        state_scr[gs, :] = s_g * jnp.concatenate(decs, axis=0) + _dot_tn(xdd[:, gs], bm_g)

    y = y_scr[...] + xs * dskip_ref[...]
    y = y * _silu(z_ref[0].astype(F32))
    for g in range(B_GROUPS):
        gs = slice(g * B_GROUP_DIM, (g + 1) * B_GROUP_DIM)
        yg = y[:, gs]
        yg = yg * lax.rsqrt(jnp.mean(yg * yg, axis=-1, keepdims=True) + RMS_EPS)
        yb_ref[0, :, gs] = (yg * normw_ref[:, gs]).astype(BF16)

    @pl.when(c == nc - 1)
    def _():
        convout_ref[0] = tail
        ssmout_ref[0] = state_scr[...].reshape(B_HEADS, B_HEAD_DIM, B_D_STATE)


def _ssd_call(xbc, z, dt, conv_init, ssm_init, prm, valid):
    b, l, _ = xbc.shape
    nc = l // B_CHUNK
    assert nc == 1 or valid == B_CHUNK
    tok = lambda bi, c: (bi, c, 0)
    per_b3 = lambda bi, c: (bi, 0, 0)
    per_b4 = lambda bi, c: (bi, 0, 0, 0)
    return pl.pallas_call(
        functools.partial(_ssd_kernel, valid=valid, nc=nc),
        grid=(b, nc),
        in_specs=[pl.BlockSpec((1, B_CHUNK, B_CONV_DIM), tok),
                  pl.BlockSpec((1, B_CHUNK, B_D_INNER), tok),
                  pl.BlockSpec((1, B_CHUNK, DT_PAD), tok),
                  pl.BlockSpec((1, B_CONV - 1, B_CONV_DIM), per_b3),
                  pl.BlockSpec((1, B_HEADS, B_HEAD_DIM, B_D_STATE), per_b4),
                  _const_spec((B_CONV, B_CONV_DIM)), _const_spec((1, B_CONV_DIM)),
                  _const_spec((1, DT_PAD)), _const_spec((1, DT_PAD)),
                  _const_spec((1, B_D_INNER)), _const_spec((1, B_D_INNER)),
                  _const_spec((LANES, B_D_INNER))],
        out_specs=[pl.BlockSpec((1, B_CHUNK, B_D_INNER), tok),
                   pl.BlockSpec((1, B_CONV - 1, B_CONV_DIM), per_b3),
                   pl.BlockSpec((1, B_HEADS, B_HEAD_DIM, B_D_STATE), per_b4)],
        out_shape=[jax.ShapeDtypeStruct((b, l, B_D_INNER), BF16),
                   jax.ShapeDtypeStruct((b, B_CONV - 1, B_CONV_DIM), F32),
                   jax.ShapeDtypeStruct((b, B_HEADS, B_HEAD_DIM, B_D_STATE), F32)],
        scratch_shapes=[pltpu.VMEM((B_D_INNER, B_D_STATE), F32),
                        pltpu.VMEM((XP_OFF + B_CHUNK, B_CONV_DIM), F32),
                        pltpu.VMEM((B_CHUNK, B_D_INNER), F32)],
        compiler_params=_params(("arbitrary", "arbitrary")),
        name="ssd",
    )(xbc, z, dt, conv_init, ssm_init, prm["conv_w"], prm["conv_b"], prm["dt_bias"], prm["a_log"],
      prm["d_skip"], prm["ssd_norm_w"], prm["expand"])


OE_SLABS = OE_COLS // LANES


def _merge_kernel(oe0_ref, oe1_ref, oe2_ref, yb_ref, hn_ref, h_ref, wgate_ref, wa_ref, wb_ref, wout_ref,
                  gpost_ref, e_ref, h2_ref, *scrs, dils):
    tm = h_ref.shape[0]
    outs, lses = [], []
    scrs = list(scrs)
    for oe, dil in zip((oe0_ref, oe1_ref, oe2_ref), dils):
        if dil == 1:
            outs.append(oe[:, :A_OUT])
            lses.append(oe[:, A_OUT:])
            continue
        scr = scrs.pop(0)
        for r in range(dil):
            for j in range(OE_SLABS):
                scr[j, pl.ds(r, tm // dil, stride=dil), :] = oe[0, r, :, j * LANES:(j + 1) * LANES]
        outs.append(jnp.concatenate([scr[j] for j in range(OE_SLABS - 1)], axis=1))
        lses.append(scr[OE_SLABS - 1])
    m = jnp.maximum(jnp.maximum(lses[0], lses[1]), lses[2])
    es = [jnp.exp(l - m) for l in lses]
    tot = es[0] + es[1] + es[2]
    o_a = None
    for o, e in zip(outs, es):
        term = _expand(e / tot, e_ref) * o
        o_a = term if o_a is None else o_a + term
    pa = _dot(o_a.astype(BF16), wa_ref[...])
    pb = _dot(yb_ref[...], wb_ref[...])
    gates = _sigmoid(_dot(hn_ref[...], wgate_ref[...]))
    merged = gates[:, :D_MODEL] * pa + gates[:, D_MODEL:] * pb
    mix = _dot(merged.astype(BF16), wout_ref[...])
    h2_ref[...] = h_ref[...] + _rms(mix, gpost_ref[...])


def _merge_call(oes, yb, hn, h, prm, tm):
    t = h.shape[0]
    row = lambda i: (i, 0)
    dils, oe_specs = [], []
    for oe in oes:
        if oe.ndim == 2:
            dils.append(1)
            oe_specs.append(pl.BlockSpec((tm, OE_COLS), row))
        else:
            dil, tpb = oe.shape[1], oe.shape[1] * oe.shape[2] // tm
            dils.append(dil)
            oe_specs.append(pl.BlockSpec((1, dil, tm // dil, OE_COLS),
                                         lambda i, tpb=tpb: (i // tpb, 0, i % tpb, 0)))
    return pl.pallas_call(
        functools.partial(_merge_kernel, dils=tuple(dils)),
        grid=(t // tm,),
        scratch_shapes=[pltpu.VMEM((OE_SLABS, tm, LANES), F32) for dil in dils if dil > 1],
        in_specs=oe_specs + [
            pl.BlockSpec((tm, B_D_INNER), row), pl.BlockSpec((tm, D_MODEL), row), pl.BlockSpec((tm, D_MODEL), row),
            _const_spec((D_MODEL, 2 * D_MODEL)), _const_spec((A_OUT, D_MODEL)),
            _const_spec((B_D_INNER, D_MODEL)), _const_spec((D_MODEL, D_MODEL)),
            _const_spec((1, D_MODEL)), _const_spec((LANES, A_OUT))],
        out_specs=pl.BlockSpec((tm, D_MODEL), row),
        out_shape=jax.ShapeDtypeStruct((t, D_MODEL), F32),
        compiler_params=_params(("parallel",)),
        name="merge",
    )(*oes, yb, hn, h, prm["w_gate"], prm["w_branch_a"], prm["w_branch_b"], prm["w_out"],
      prm["g_post_mix"], prm["expand8"])


def _expand_matrix():
    e = np.zeros((LANES, B_D_INNER), np.float32)
    for h in range(B_HEADS):
        e[h, h * B_HEAD_DIM:(h + 1) * B_HEAD_DIM] = 1.0
    return e


def _prepare(w_in, conv_w, conv_b, dt_bias, a_log, d_skip, ssd_norm_w, w_branch_a, w_branch_b, w_out,
             ffn1_gate, ffn1_up, ffn1_down, ffn2_gate, ffn2_up, ffn2_down,
             g_pre_ffn1, g_post_ffn1, g_pre_mix, g_post_mix, g_pre_ffn2, g_post_ffn2):
    lane_pad = lambda v: jnp.pad(v, (0, DT_PAD - v.shape[0]))[None, :]
    w_zxd = jnp.pad(w_in[:, OFF_Z:OFF_GATE], ((0, 0), (0, DT_PAD - B_HEADS)))
    e = _expand_matrix()
    return {
        "w_qkv": w_in[:, :OFF_Z].astype(BF16),
        "w_kv_t": [w_in[:, g * QKV_COLS + A_OUT:(g + 1) * QKV_COLS].T.astype(BF16) for g in range(N_A_GROUPS)],
        "w_zxd": w_zxd.astype(BF16),
        "w_gate": w_in[:, OFF_GATE:].astype(BF16),
        "conv_w": conv_w, "conv_b": conv_b[None, :],
        "dt_bias": lane_pad(dt_bias), "a_log": lane_pad(a_log),
        "d_skip": jnp.repeat(d_skip, B_HEAD_DIM)[None, :],
        "ssd_norm_w": ssd_norm_w[None, :],
        "expand": jnp.asarray(e, BF16), "expand8": jnp.asarray(e[:, :A_OUT], BF16),
        "w_branch_a": w_branch_a.astype(BF16), "w_branch_b": w_branch_b.astype(BF16), "w_out": w_out.astype(BF16),
        "ffn1": (ffn1_gate.astype(BF16), ffn1_up.astype(BF16), ffn1_down.astype(BF16)),
        "ffn2": (ffn2_gate.astype(BF16), ffn2_up.astype(BF16), ffn2_down.astype(BF16)),
        "g_pre_ffn1": g_pre_ffn1[None, :], "g_post_ffn1": g_post_ffn1[None, :],
        "g_pre_mix": g_pre_mix[None, :], "g_post_mix": g_post_mix[None, :],
        "g_pre_ffn2": g_pre_ffn2[None, :], "g_post_ffn2": g_post_ffn2[None, :],
    }


ZXD_SPLITS = ((0, B_D_INNER), (B_D_INNER, B_D_INNER + B_CONV_DIM), (B_D_INNER + B_CONV_DIM, B_D_INNER + B_CONV_DIM + DT_PAD))


def _prompt_layer(x, prm, tm, tq):
    b, s, _ = x.shape
    t = b * s
    h1, hn = _ffn_call(x.reshape(t, D_MODEL), prm["g_pre_ffn1"], prm["g_post_ffn1"], prm["g_pre_mix"], *prm["ffn1"], FFN_ROW_TILE)
    qkvs = _proj_qkv_call(hn, prm["w_qkv"], b, s, tm)
    z, xbc, dt = _proj_call(hn, prm["w_zxd"], ZXD_SPLITS, (BF16, F32, F32), tm, "proj_zxd")

    kv_tails = []
    for g, (window, _) in enumerate(A_GROUPS):
        rows = min(window, s)
        kvt = _proj_t_call(hn, prm["w_kv_t"][g], b, s, rows, min(rows, tm), f"proj_kv_g{g}")
        kvt = kvt.reshape(b, 2, A_HEADS, A_HEAD_DIM, rows)
        kv_tails.append(jnp.transpose(kvt, (0, 4, 1, 2, 3))[None])

    oes = [_attn_prompt_call(qkvs[g], tq, f"attn_prompt_g{g}") for g in range(N_A_GROUPS)]
    oes[0] = oes[0].reshape(t, OE_COLS)
    yb, conv_out, ssm_out = _ssd_call(
        xbc.reshape(b, s, B_CONV_DIM), z.reshape(b, s, B_D_INNER), dt.reshape(b, s, DT_PAD),
        jnp.zeros((b, B_CONV - 1, B_CONV_DIM), F32), jnp.zeros((b, B_HEADS, B_HEAD_DIM, B_D_STATE), F32),
        prm, B_CHUNK)
    h2 = _merge_call(oes, yb.reshape(t, B_D_INNER), hn, h1, prm, tm)
    (y,) = _ffn_call(h2, prm["g_pre_ffn2"], prm["g_post_ffn2"], None, *prm["ffn2"], FFN_ROW_TILE)
    return y.reshape(b, s, D_MODEL), kv_tails, conv_out[None], ssm_out[None]


def _sample_layer(x, caches, conv_state, ssm_state, prm):
    b, l, _ = x.shape
    t = b * l
    h1, hn = _ffn_call(x.reshape(t, D_MODEL), prm["g_pre_ffn1"], prm["g_post_ffn1"], prm["g_pre_mix"], *prm["ffn1"], t)
    (qkv,) = _proj_call(hn, prm["w_qkv"], ((0, OFF_Z),), (F32,), t, "proj_qkv_s")
    z, xbc, dt = _proj_call(hn, prm["w_zxd"], ZXD_SPLITS, (BF16, F32, F32), t, "proj_zxd_s")

    qkv5 = qkv.reshape(b, l, N_A_GROUPS, 3, A_OUT)
    new_caches, oes = [], []
    for g in range(N_A_GROUPS):
        newbuf, o, lse = _attn_sample_call(caches[g], qkv5[:, :, g, 0], qkv5[:, :, g, 1], qkv5[:, :, g, 2], g)
        new_caches.append(newbuf[None])
        oes.append(jnp.concatenate([o, lse], axis=-1).reshape(t, OE_COLS))

    pad = lambda a: jnp.pad(a.reshape(b, l, a.shape[-1]), ((0, 0), (0, B_CHUNK - l), (0, 0)))
    yb, conv_out, ssm_out = _ssd_call(pad(xbc), pad(z), pad(dt), conv_state, ssm_state, prm, l)
    h2 = _merge_call(oes, yb[:, :l].reshape(t, B_D_INNER), hn, h1, prm, t)
    (y,) = _ffn_call(h2, prm["g_pre_ffn2"], prm["g_post_ffn2"], None, *prm["ffn2"], t)
    return y.reshape(b, l, D_MODEL), new_caches, conv_out[None], ssm_out[None]


PROMPT_ROW_TILE = 512
FFN_ROW_TILE = 1024
PROMPT_QUERY_TILE = 512


def kernel(x_prompt, x_sample, cache_kv_w128, cache_kv_w512, cache_kv_w2048, state_conv, state_ssm, w_in, conv_w, conv_b, dt_bias, a_log, d_skip, ssd_norm_w, w_branch_a, w_branch_b, w_out, ffn1_gate, ffn1_up, ffn1_down, ffn2_gate, ffn2_up, ffn2_down, g_pre_ffn1, g_post_ffn1, g_pre_mix, g_post_mix, g_pre_ffn2, g_post_ffn2):
    assert w_in.shape[0] == 1, "single-layer trunk"
    prm = _prepare(*(p[0] for p in (
        w_in, conv_w, conv_b, dt_bias, a_log, d_skip, ssd_norm_w, w_branch_a, w_branch_b, w_out,
        ffn1_gate, ffn1_up, ffn1_down, ffn2_gate, ffn2_up, ffn2_down,
        g_pre_ffn1, g_post_ffn1, g_pre_mix, g_post_mix, g_pre_ffn2, g_post_ffn2)))
    y_p, kv_p, conv_p, ssm_p = _prompt_layer(x_prompt, prm, PROMPT_ROW_TILE, PROMPT_QUERY_TILE)
    y_s, kv_s, conv_s, ssm_s = _sample_layer(
        x_sample, (cache_kv_w128[0], cache_kv_w512[0], cache_kv_w2048[0]), state_conv[0], state_ssm[0], prm)
    return (y_p, y_s, kv_p[0], kv_p[1], kv_p[2], conv_p, ssm_p, kv_s[0], kv_s[1], kv_s[2], conv_s, ssm_s)
```

```python
import functools

import jax
import jax.numpy as jnp
import numpy as np
from jax import lax
from jax.experimental import pallas as pl
from jax.experimental.pallas import tpu as pltpu

F32 = jnp.float32
BF16 = jnp.bfloat16

D_MODEL = 1024
A_GROUPS = ((128, 1), (512, 4), (2048, 16))
N_A_GROUPS = len(A_GROUPS)
A_HEADS = 8
A_HEAD_DIM = 64
A_OUT = A_HEADS * A_HEAD_DIM
A_BAND = 128
B_D_INNER = 1536
B_HEAD_DIM = 64
B_HEADS = B_D_INNER // B_HEAD_DIM
B_GROUPS = 4
B_GROUP_HEADS = B_HEADS // B_GROUPS
B_GROUP_DIM = B_D_INNER // B_GROUPS
B_D_STATE = 128
B_CONV = 4
B_CHUNK = 128
B_CONV_DIM = B_D_INNER + 2 * B_GROUPS * B_D_STATE
D_FF = 2816
RMS_EPS = 1e-6

OFF_Z = N_A_GROUPS * 3 * A_OUT
OFF_XBC = OFF_Z + B_D_INNER
OFF_DT = OFF_XBC + B_CONV_DIM
OFF_GATE = OFF_DT + B_HEADS

LANES = 128
DT_PAD = LANES
OE_COLS = A_OUT + LANES
FF_CHUNK = 256
VMEM_LIMIT = 56 * 1024 * 1024
NEG_INF = float("-inf")


def _const_spec(shape):
    nd = len(shape)
    return pl.BlockSpec(shape, lambda *_: (0,) * nd, pipeline_mode=pl.Buffered(1))


def _params(sem):
    return pltpu.CompilerParams(dimension_semantics=sem, vmem_limit_bytes=VMEM_LIMIT)


def _rms(x, g):
    return x * lax.rsqrt(jnp.mean(x * x, axis=-1, keepdims=True) + RMS_EPS) * g


def _sigmoid(x):
    return 0.5 * jnp.tanh(0.5 * x) + 0.5


def _silu(x):
    return x * _sigmoid(x)


def _dot(a, b):
    return jnp.dot(a, b, preferred_element_type=F32)


def _dot_nt(a, b):
    return lax.dot_general(a, b, (((1,), (1,)), ((), ())), preferred_element_type=F32)


def _dot_tn(a, b):
    return lax.dot_general(a, b, (((0,), (0,)), ((), ())), preferred_element_type=F32)


def _split_bf16(a, parts):
    out = []
    r = a
    for i in range(parts):
        p = r.astype(BF16)
        out.append(p)
        if i + 1 < parts:
            r = r - p.astype(F32)
    return out


def _expand(a, e_ref):
    e = e_ref[...]
    hi, lo = _split_bf16(a, 2)
    return _dot(hi, e) + _dot(lo, e)


def _ffn_kernel(x_ref, gpre_ref, gpost_ref, gnext_ref, wg_ref, wu_ref, wd_ref, *rest, emit_next):
    if emit_next:
        h_ref, hn_ref, a_scr = rest
    else:
        h_ref, a_scr = rest
    x = x_ref[...]
    xn = _rms(x, gpre_ref[...]).astype(BF16)
    for c in range(D_FF // FF_CHUNK):
        sl = slice(c * FF_CHUNK, (c + 1) * FF_CHUNK)
        g = _dot(xn, wg_ref[:, sl])
        u = _dot(xn, wu_ref[:, sl])
        a_scr[:, sl] = (_silu(g) * u).astype(BF16)
    f = _dot(a_scr[...], wd_ref[...])
    h = x + 0.5 * _rms(f, gpost_ref[...])
    h_ref[...] = h
    if emit_next:
        hn_ref[...] = _rms(h, gnext_ref[...]).astype(BF16)


def _ffn_call(x, g_pre, g_post, g_next, wg, wu, wd, tm):
    t = x.shape[0]
    emit_next = g_next is not None
    row = lambda i: (i, 0)
    out_shape = [jax.ShapeDtypeStruct((t, D_MODEL), F32)]
    out_specs = [pl.BlockSpec((tm, D_MODEL), row)]
    if emit_next:
        out_shape.append(jax.ShapeDtypeStruct((t, D_MODEL), BF16))
        out_specs.append(pl.BlockSpec((tm, D_MODEL), row))
    return pl.pallas_call(
        functools.partial(_ffn_kernel, emit_next=emit_next),
        grid=(t // tm,),
        in_specs=[pl.BlockSpec((tm, D_MODEL), row),
                  _const_spec((1, D_MODEL)), _const_spec((1, D_MODEL)), _const_spec((1, D_MODEL)),
                  _const_spec((D_MODEL, D_FF)), _const_spec((D_MODEL, D_FF)), _const_spec((D_FF, D_MODEL))],
        out_specs=out_specs,
        out_shape=out_shape,
        scratch_shapes=[pltpu.VMEM((tm, D_FF), BF16)],
        compiler_params=_params(("parallel",)),
        name="ffn_next" if emit_next else "ffn",
    )(x, g_pre, g_post, g_next if emit_next else g_post, wg, wu, wd)


PROJ_CHUNK = 1536


def _proj_kernel(x_ref, w_ref, *o_refs, splits):
    x = x_ref[...]
    for o_ref, (a, b) in zip(o_refs, splits):
        for c0 in range(a, b, PROJ_CHUNK):
            c1 = min(b, c0 + PROJ_CHUNK)
            o_ref[:, c0 - a:c1 - a] = _dot(x, w_ref[:, c0:c1]).astype(o_ref.dtype)


def _proj_call(x, w, splits, dtypes, tm, name, rows=None, row_block=None):
    t, k = x.shape
    rows = t if rows is None else rows
    row_block = (lambda i: i) if row_block is None else row_block
    return pl.pallas_call(
        functools.partial(_proj_kernel, splits=splits),
        grid=(rows // tm,),
        in_specs=[pl.BlockSpec((tm, k), lambda i: (row_block(i), 0)), _const_spec(w.shape)],
        out_specs=[pl.BlockSpec((tm, b - a), lambda i: (i, 0)) for a, b in splits],
        out_shape=[jax.ShapeDtypeStruct((rows, b - a), dt) for (a, b), dt in zip(splits, dtypes)],
        compiler_params=_params(("parallel",)),
        name=name,
    )(x, w)


def _proj_t_kernel(x_ref, wt_ref, o_ref):
    o_ref[0] = _dot_nt(wt_ref[...], x_ref[...])


def _proj_t_call(x, wt, b, s, rows, tm, name):
    n, k = wt.shape
    per_b, first, sb = rows // tm, (s - rows) // tm, s // tm
    return pl.pallas_call(
        _proj_t_kernel,
        grid=(b, per_b),
        in_specs=[pl.BlockSpec((tm, k), lambda bi, j: (bi * sb + first + j, 0)), _const_spec(wt.shape)],
        out_specs=pl.BlockSpec((1, n, tm), lambda bi, j: (bi, 0, j)),
        out_shape=jax.ShapeDtypeStruct((b, n, rows), F32),
        compiler_params=_params(("parallel", "parallel")),
        name=name,
    )(x, wt)


QKV_COLS = 3 * A_OUT
SLABS = A_OUT // LANES


def _proj_qkv_kernel(x_ref, w_ref, *refs):
    o_refs, scrs = refs[:N_A_GROUPS], refs[N_A_GROUPS:]
    x = x_ref[...]
    tm = x.shape[0]
    for g, (o_ref, (_, dil)) in enumerate(zip(o_refs, A_GROUPS)):
        for part in range(3):
            c0 = part * A_OUT
            res = _dot(x, w_ref[:, g * QKV_COLS + c0:g * QKV_COLS + c0 + A_OUT])
            if dil == 1:
                o_ref[0, 0, :, c0:c0 + A_OUT] = res.astype(BF16)
                continue
            scr = scrs[g - 1]
            for j in range(SLABS):
                scr[part * SLABS + j] = res[:, j * LANES:(j + 1) * LANES]
            for r in range(dil):
                for j in range(SLABS):
                    rows = scr[part * SLABS + j, pl.ds(r, tm // dil, stride=dil), :]
                    o_ref[0, r, :, c0 + j * LANES:c0 + (j + 1) * LANES] = rows.astype(BF16)


def _proj_qkv_call(x, w, b, s, tm):
    t, k = x.shape
    tpb = s // tm
    dils = [dil for _, dil in A_GROUPS]
    return pl.pallas_call(
        _proj_qkv_kernel,
        grid=(t // tm,),
        in_specs=[pl.BlockSpec((tm, k), lambda i: (i, 0)), _const_spec(w.shape)],
        out_specs=[pl.BlockSpec((1, dil, tm // dil, QKV_COLS), lambda i: (i // tpb, 0, i % tpb, 0)) for dil in dils],
        out_shape=[jax.ShapeDtypeStruct((b, dil, s // dil, QKV_COLS), BF16) for dil in dils],
        scratch_shapes=[pltpu.VMEM((3 * SLABS, tm, LANES), F32) for dil in dils if dil > 1],
        compiler_params=_params(("parallel",)),
        name="proj_qkv",
    )(x, w)


def _attn_prompt_kernel(q_ref, kp_ref, kc_ref, vp_ref, vc_ref, o_ref, k_scr, v_scr, *, tq):
    n = pl.program_id(2)
    k_scr[0:A_BAND, :] = kp_ref[0, 0]
    k_scr[A_BAND:, :] = kc_ref[0, 0]
    v_scr[0:A_BAND, :] = vp_ref[0, 0]
    v_scr[A_BAND:, :] = vc_ref[0, 0]
    qi = lax.broadcasted_iota(jnp.int32, (A_BAND, 2 * A_BAND), 0)
    kj = lax.broadcasted_iota(jnp.int32, (A_BAND, 2 * A_BAND), 1)
    dist = qi + A_BAND - kj
    band = (dist >= 0) & (dist <= A_BAND)
    own = kj >= A_BAND
    lane = lax.broadcasted_iota(jnp.int32, (A_BAND, LANES), 1)
    low = lane < A_HEAD_DIM
    lane_row = lax.broadcasted_iota(jnp.int32, (1, LANES), 1)
    head_mask = ((lane_row < A_HEAD_DIM).astype(BF16), (lane_row >= A_HEAD_DIM).astype(BF16))

    def block(i, carry):
        r0 = pl.multiple_of(i * A_BAND, A_BAND)
        valid = band & (own | (n > 0) | (i > 0))
        lse_blk = jnp.zeros((A_BAND, LANES), F32)
        for j in range(A_OUT // LANES):
            cs = slice(j * LANES, (j + 1) * LANES)
            q2 = q_ref[0, 0, pl.ds(r0, A_BAND), cs]
            k2 = k_scr[pl.ds(r0, 2 * A_BAND), cs]
            v2 = v_scr[pl.ds(r0, 2 * A_BAND), cs]
            outs = []
            for half in range(2):
                s = _dot_nt(q2 * head_mask[half], k2) * (A_HEAD_DIM ** -0.5)
                s = jnp.where(valid, s, NEG_INF)
                m = jnp.max(s, axis=-1, keepdims=True)
                p = jnp.exp(s - m)
                den = jnp.sum(p, axis=-1, keepdims=True)
                outs.append(_dot(p.astype(BF16), v2) / den)
                lse_blk = jnp.where(lane == 2 * j + half, m + jnp.log(den), lse_blk)
            o_ref[0, 0, pl.ds(r0, A_BAND), cs] = jnp.where(low, outs[0], outs[1])
        o_ref[0, 0, pl.ds(r0, A_BAND), A_OUT:] = lse_blk
        return carry

    lax.fori_loop(0, tq // A_BAND, block, 0)


def _attn_prompt_call(qkv, tq, name):
    b, dil, ls, _ = qkv.shape
    tq = min(tq, ls)
    per = tq // A_BAND

    def cur(j):
        return lambda bi, r, n: (bi, r, n, j)

    def prev(j):
        return lambda bi, r, n: (bi, r, jnp.maximum(n * per - 1, 0), j)

    return pl.pallas_call(
        functools.partial(_attn_prompt_kernel, tq=tq),
        grid=(b, dil, ls // tq),
        in_specs=[pl.BlockSpec((1, 1, tq, A_OUT), cur(0)),
                  pl.BlockSpec((1, 1, A_BAND, A_OUT), prev(1)), pl.BlockSpec((1, 1, tq, A_OUT), cur(1)),
                  pl.BlockSpec((1, 1, A_BAND, A_OUT), prev(2)), pl.BlockSpec((1, 1, tq, A_OUT), cur(2))],
        out_specs=pl.BlockSpec((1, 1, tq, OE_COLS), lambda bi, r, n: (bi, r, n, 0)),
        out_shape=jax.ShapeDtypeStruct((b, dil, ls, OE_COLS), F32),
        scratch_shapes=[pltpu.VMEM((tq + A_BAND, A_OUT), BF16), pltpu.VMEM((tq + A_BAND, A_OUT), BF16)],
        compiler_params=_params(("parallel", "parallel", "arbitrary")),
        name=name,
    )(qkv, qkv, qkv, qkv, qkv)


S_ROWS = 128
NEW_ROWS = 16


def _attn_sample_kernel(ct_ref, kvnew_ref, q_ref, newct_ref, o_ref, lse_ref, *, window, dil, n_new):
    w = window
    ct = ct_ref[0]
    new = kvnew_ref[0]
    new_t = jnp.concatenate([new, jnp.zeros((LANES - NEW_ROWS, 2 * A_OUT), F32)], axis=0).T
    newct_ref[0] = jnp.concatenate([ct[:, n_new:], new_t[:, :n_new]], axis=1)

    r = lax.broadcasted_iota(jnp.int32, (S_ROWS, A_OUT), 0)
    c = lax.broadcasted_iota(jnp.int32, (S_ROWS, A_OUT), 1)
    head_lanes = (r // 8) == (c // A_HEAD_DIM)
    qm = jnp.where(head_lanes, q_ref[0], 0.0).astype(BF16)
    new_b = new.astype(BF16)
    scale = A_HEAD_DIM ** -0.5
    s_c = _dot(qm, ct[:A_OUT].astype(BF16)) * scale
    s_n = _dot_nt(qm, new_b[:, :A_OUT]) * scale

    def mask(s, first):
        rr = lax.broadcasted_iota(jnp.int32, s.shape, 0)
        pos = lax.broadcasted_iota(jnp.int32, s.shape, 1) + first
        d = w + (rr & (n_new - 1)) - pos
        return jnp.where((d >= 0) & (d <= window) & ((d & (dil - 1)) == 0), s, NEG_INF)

    s_c = mask(s_c, 0)
    s_n = mask(s_n, w)
    m = jnp.maximum(jnp.max(s_c, axis=-1, keepdims=True), jnp.max(s_n, axis=-1, keepdims=True))
    p_c = jnp.exp(s_c - m)
    p_n = jnp.exp(s_n - m)
    den = jnp.sum(p_c, axis=-1, keepdims=True) + jnp.sum(p_n, axis=-1, keepdims=True)
    pv = _dot_nt(p_c.astype(BF16), ct[A_OUT:].astype(BF16)) + _dot(p_n.astype(BF16), new_b[:, A_OUT:])
    o = jnp.where(head_lanes, pv / den, 0.0)
    lse = m + jnp.log(den)
    r2 = lax.broadcasted_iota(jnp.int32, (S_ROWS, LANES), 0)
    c2 = lax.broadcasted_iota(jnp.int32, (S_ROWS, LANES), 1)
    lse_sel = jnp.where((r2 // 8) == c2, lse, 0.0)
    o_acc = o[0:8]
    lse_acc = lse_sel[0:8]
    for h in range(1, A_HEADS):
        o_acc = o_acc + o[8 * h:8 * h + 8]
        lse_acc = lse_acc + lse_sel[8 * h:8 * h + 8]
    o_ref[0] = o_acc
    lse_ref[0] = lse_acc


def _attn_sample_call(cache, q, k, v, group):
    b, lb = cache.shape[:2]
    window, dil = A_GROUPS[group]
    n_new = q.shape[1]
    assert lb == window and n_new == 4
    ct = jnp.transpose(cache, (0, 2, 3, 4, 1)).reshape(b, 2 * A_OUT, lb)
    kvnew = jnp.pad(jnp.concatenate([k, v], axis=-1), ((0, 0), (0, NEW_ROWS - n_new), (0, 0)))
    q8 = jnp.pad(q, ((0, 0), (0, 8 - n_new), (0, 0)))
    q_rows = jnp.pad(jnp.tile(q8, (1, A_HEADS, 1)), ((0, 0), (0, S_ROWS - 8 * A_HEADS), (0, 0)))
    batch = lambda i: (i, 0, 0)
    newct, o, lse = pl.pallas_call(
        functools.partial(_attn_sample_kernel, window=window, dil=dil, n_new=n_new),
        grid=(b,),
        in_specs=[pl.BlockSpec((1, 2 * A_OUT, lb), batch),
                  pl.BlockSpec((1, NEW_ROWS, 2 * A_OUT), batch),
                  pl.BlockSpec((1, S_ROWS, A_OUT), batch)],
        out_specs=[pl.BlockSpec((1, 2 * A_OUT, window), batch),
                   pl.BlockSpec((1, 8, A_OUT), batch),
                   pl.BlockSpec((1, 8, LANES), batch)],
        out_shape=[jax.ShapeDtypeStruct((b, 2 * A_OUT, window), F32),
                   jax.ShapeDtypeStruct((b, 8, A_OUT), F32),
                   jax.ShapeDtypeStruct((b, 8, LANES), F32)],
        compiler_params=_params(("parallel",)),
        name=f"attn_sample_g{group}",
    )(ct, kvnew, q_rows)
    newbuf = jnp.transpose(newct.reshape(b, 2, A_HEADS, A_HEAD_DIM, window), (0, 4, 1, 2, 3))
    return newbuf, o[:, :n_new], lse[:, :n_new]


XP_OFF = 8


def _ssd_kernel(xbc_ref, z_ref, dt_ref, cinit_ref, sinit_ref, convw_ref, convb_ref, dtb_ref, alog_ref,
                dskip_ref, normw_ref, e_ref, yb_ref, convout_ref, ssmout_ref, state_scr, xp_scr, y_scr, *, valid, nc):
    t = B_CHUNK
    c = pl.program_id(1)

    @pl.when(c == 0)
    def _():
        state_scr[...] = sinit_ref[0].reshape(B_D_INNER, B_D_STATE)
        xp_scr[XP_OFF - (B_CONV - 1):XP_OFF, :] = cinit_ref[0]

    xp_scr[XP_OFF:XP_OFF + t, :] = xbc_ref[0]
    conv = convb_ref[...]
    for tap in range(B_CONV):
        lo = XP_OFF - (B_CONV - 1) + tap
        conv = conv + xp_scr[lo:lo + t, :] * convw_ref[tap:tap + 1, :]
    tail = xp_scr[XP_OFF + valid - (B_CONV - 1):XP_OFF + valid, :]
    xp_scr[XP_OFF - (B_CONV - 1):XP_OFF, :] = tail

    xc = _silu(conv)
    xs = xc[:, :B_D_INNER]
    row = lax.broadcasted_iota(jnp.int32, (t, t), 0)
    col = lax.broadcasted_iota(jnp.int32, (t, t), 1)
    causal = row >= col
    x_dt = dt_ref[0] + dtb_ref[...]
    dt = jnp.maximum(x_dt, 0.0) + jnp.log1p(jnp.exp(-jnp.abs(x_dt)))
    if valid < t:
        dt = jnp.where(row < valid, dt, 0.0)
    da = dt * (-jnp.exp(alog_ref[...]))
    tri = jnp.where(causal, 1.0, 0.0).astype(BF16)
    cs = sum(_dot(tri, part) for part in _split_bf16(da, 3))
    cs_t = cs.T
    dt_t = dt.T
    cs_last = cs[t - 1:t, :]
    both = jnp.concatenate([jnp.exp(cs), dt * jnp.exp(cs_last - cs)], axis=0)
    both_x = _expand(both, e_ref)
    ecs_x = both_x[:t]
    xdd = (xs * both_x[t:]).astype(BF16)
    xs_b = xs.astype(BF16)
    dec_t = jnp.exp(cs_t[:, t - 1:t])

    for g in range(B_GROUPS):
        bm_g = xc[:, B_D_INNER + g * B_D_STATE:B_D_INNER + (g + 1) * B_D_STATE].astype(BF16)
        cm_lo = B_D_INNER + (B_GROUPS + g) * B_D_STATE
        cm_g = xc[:, cm_lo:cm_lo + B_D_STATE].astype(BF16)
        cb = _dot_nt(cm_g, bm_g)
        gs = slice(g * B_GROUP_DIM, (g + 1) * B_GROUP_DIM)
        s_g = state_scr[gs, :]
        y_off = _dot_nt(cm_g, s_g.astype(BF16)) * ecs_x[:, gs]
        decs = []
        for r in range(B_GROUP_HEADS):
            h = g * B_GROUP_HEADS + r
            hs = slice(h * B_HEAD_DIM, (h + 1) * B_HEAD_DIM)
            seg = cs[:, h:h + 1] - cs_t[h:h + 1, :]
            lmat = jnp.exp(jnp.where(causal, seg, NEG_INF))
            mat = (cb * lmat * dt_t[h:h + 1, :]).astype(BF16)
            y_scr[:, hs] = _dot(mat, xs_b[:, hs]) + y_off[:, r * B_HEAD_DIM:(r + 1) * B_HEAD_DIM]
            decs.append(jnp.broadcast_to(dec_t[h:h + 1, :], (B_HEAD_DIM, B_D_STATE)))
        state_scr[gs, :] = s_g * jnp.concatenate(decs, axis=0) + _dot_tn(xdd[:, gs], bm_g)

    y = y_scr[...] + xs * dskip_ref[...]
    y = y * _silu(z_ref[0].astype(F32))
    for g in range(B_GROUPS):
        gs = slice(g * B_GROUP_DIM, (g + 1) * B_GROUP_DIM)
        yg = y[:, gs]
        yg = yg * lax.rsqrt(jnp.mean(yg * yg, axis=-1, keepdims=True) + RMS_EPS)
        yb_ref[0, :, gs] = (yg * normw_ref[:, gs]).astype(BF16)

    @pl.when(c == nc - 1)
    def _():
        convout_ref[0] = tail
        ssmout_ref[0] = state_scr[...].reshape(B_HEADS, B_HEAD_DIM, B_D_STATE)


def _ssd_call(xbc, z, dt, conv_init, ssm_init, prm, valid):
    b, l, _ = xbc.shape
    nc = l // B_CHUNK
    assert nc == 1 or valid == B_CHUNK
    tok = lambda bi, c: (bi, c, 0)
    per_b3 = lambda bi, c: (bi, 0, 0)
    per_b4 = lambda bi, c: (bi, 0, 0, 0)
    return pl.pallas_call(
        functools.partial(_ssd_kernel, valid=valid, nc=nc),
        grid=(b, nc),
        in_specs=[pl.BlockSpec((1, B_CHUNK, B_CONV_DIM), tok),
                  pl.BlockSpec((1, B_CHUNK, B_D_INNER), tok),
                  pl.BlockSpec((1, B_CHUNK, DT_PAD), tok),
                  pl.BlockSpec((1, B_CONV - 1, B_CONV_DIM), per_b3),
                  pl.BlockSpec((1, B_HEADS, B_HEAD_DIM, B_D_STATE), per_b4),
                  _const_spec((B_CONV, B_CONV_DIM)), _const_spec((1, B_CONV_DIM)),
                  _const_spec((1, DT_PAD)), _const_spec((1, DT_PAD)),
                  _const_spec((1, B_D_INNER)), _const_spec((1, B_D_INNER)),
                  _const_spec((LANES, B_D_INNER))],
        out_specs=[pl.BlockSpec((1, B_CHUNK, B_D_INNER), tok),
                   pl.BlockSpec((1, B_CONV - 1, B_CONV_DIM), per_b3),
                   pl.BlockSpec((1, B_HEADS, B_HEAD_DIM, B_D_STATE), per_b4)],
        out_shape=[jax.ShapeDtypeStruct((b, l, B_D_INNER), BF16),
                   jax.ShapeDtypeStruct((b, B_CONV - 1, B_CONV_DIM), F32),
                   jax.ShapeDtypeStruct((b, B_HEADS, B_HEAD_DIM, B_D_STATE), F32)],
        scratch_shapes=[pltpu.VMEM((B_D_INNER, B_D_STATE), F32),
                        pltpu.VMEM((XP_OFF + B_CHUNK, B_CONV_DIM), F32),
                        pltpu.VMEM((B_CHUNK, B_D_INNER), F32)],
        compiler_params=_params(("arbitrary", "arbitrary")),
        name="ssd",
    )(xbc, z, dt, conv_init, ssm_init, prm["conv_w"], prm["conv_b"], prm["dt_bias"], prm["a_log"],
      prm["d_skip"], prm["ssd_norm_w"], prm["expand"])


OE_SLABS = OE_COLS // LANES


def _merge_kernel(oe0_ref, oe1_ref, oe2_ref, yb_ref, hn_ref, h_ref, wgate_ref, wa_ref, wb_ref, wout_ref,
                  gpost_ref, e_ref, h2_ref, *scrs, dils):
    tm = h_ref.shape[0]
    outs, lses = [], []
    scrs = list(scrs)
    for oe, dil in zip((oe0_ref, oe1_ref, oe2_ref), dils):
        if dil == 1:
            outs.append(oe[:, :A_OUT])
            lses.append(oe[:, A_OUT:])
            continue
        scr = scrs.pop(0)
        for r in range(dil):
            for j in range(OE_SLABS):
                scr[j, pl.ds(r, tm // dil, stride=dil), :] = oe[0, r, :, j * LANES:(j + 1) * LANES]
        outs.append(jnp.concatenate([scr[j] for j in range(OE_SLABS - 1)], axis=1))
        lses.append(scr[OE_SLABS - 1])
    m = jnp.maximum(jnp.maximum(lses[0], lses[1]), lses[2])
    es = [jnp.exp(l - m) for l in lses]
    tot = es[0] + es[1] + es[2]
    o_a = None
    for o, e in zip(outs, es):
        term = _expand(e / tot, e_ref) * o
        o_a = term if o_a is None else o_a + term
    pa = _dot(o_a.astype(BF16), wa_ref[...])
    pb = _dot(yb_ref[...], wb_ref[...])
    gates = _sigmoid(_dot(hn_ref[...], wgate_ref[...]))
    merged = gates[:, :D_MODEL] * pa + gates[:, D_MODEL:] * pb
    mix = _dot(merged.astype(BF16), wout_ref[...])
    h2_ref[...] = h_ref[...] + _rms(mix, gpost_ref[...])


def _merge_call(oes, yb, hn, h, prm, tm):
    t = h.shape[0]
    row = lambda i: (i, 0)
    dils, oe_specs = [], []
    for oe in oes:
        if oe.ndim == 2:
            dils.append(1)
            oe_specs.append(pl.BlockSpec((tm, OE_COLS), row))
        else:
            dil, tpb = oe.shape[1], oe.shape[1] * oe.shape[2] // tm
            dils.append(dil)
            oe_specs.append(pl.BlockSpec((1, dil, tm // dil, OE_COLS),
                                         lambda i, tpb=tpb: (i // tpb, 0, i % tpb, 0)))
    return pl.pallas_call(
        functools.partial(_merge_kernel, dils=tuple(dils)),
        grid=(t // tm,),
        scratch_shapes=[pltpu.VMEM((OE_SLABS, tm, LANES), F32) for dil in dils if dil > 1],
        in_specs=oe_specs + [
            pl.BlockSpec((tm, B_D_INNER), row), pl.BlockSpec((tm, D_MODEL), row), pl.BlockSpec((tm, D_MODEL), row),
            _const_spec((D_MODEL, 2 * D_MODEL)), _const_spec((A_OUT, D_MODEL)),
            _const_spec((B_D_INNER, D_MODEL)), _const_spec((D_MODEL, D_MODEL)),
            _const_spec((1, D_MODEL)), _const_spec((LANES, A_OUT))],
        out_specs=pl.BlockSpec((tm, D_MODEL), row),
        out_shape=jax.ShapeDtypeStruct((t, D_MODEL), F32),
        compiler_params=_params(("parallel",)),
        name="merge",
    )(*oes, yb, hn, h, prm["w_gate"], prm["w_branch_a"], prm["w_branch_b"], prm["w_out"],
      prm["g_post_mix"], prm["expand8"])


def _expand_matrix():
    e = np.zeros((LANES, B_D_INNER), np.float32)
    for h in range(B_HEADS):
        e[h, h * B_HEAD_DIM:(h + 1) * B_HEAD_DIM] = 1.0
    return e


def _prepare(w_in, conv_w, conv_b, dt_bias, a_log, d_skip, ssd_norm_w, w_branch_a, w_branch_b, w_out,
             ffn1_gate, ffn1_up, ffn1_down, ffn2_gate, ffn2_up, ffn2_down,
             g_pre_ffn1, g_post_ffn1, g_pre_mix, g_post_mix, g_pre_ffn2, g_post_ffn2):
    lane_pad = lambda v: jnp.pad(v, (0, DT_PAD - v.shape[0]))[None, :]
    w_zxd = jnp.pad(w_in[:, OFF_Z:OFF_GATE], ((0, 0), (0, DT_PAD - B_HEADS)))
    e = _expand_matrix()
    return {
        "w_qkv": w_in[:, :OFF_Z].astype(BF16),
        "w_kv_t": [w_in[:, g * QKV_COLS + A_OUT:(g + 1) * QKV_COLS].T.astype(BF16) for g in range(N_A_GROUPS)],
        "w_zxd": w_zxd.astype(BF16),
        "w_gate": w_in[:, OFF_GATE:].astype(BF16),
        "conv_w": conv_w, "conv_b": conv_b[None, :],
        "dt_bias": lane_pad(dt_bias), "a_log": lane_pad(a_log),
        "d_skip": jnp.repeat(d_skip, B_HEAD_DIM)[None, :],
        "ssd_norm_w": ssd_norm_w[None, :],
        "expand": jnp.asarray(e, BF16), "expand8": jnp.asarray(e[:, :A_OUT], BF16),
        "w_branch_a": w_branch_a.astype(BF16), "w_branch_b": w_branch_b.astype(BF16), "w_out": w_out.astype(BF16),
        "ffn1": (ffn1_gate.astype(BF16), ffn1_up.astype(BF16), ffn1_down.astype(BF16)),
        "ffn2": (ffn2_gate.astype(BF16), ffn2_up.astype(BF16), ffn2_down.astype(BF16)),
        "g_pre_ffn1": g_pre_ffn1[None, :], "g_post_ffn1": g_post_ffn1[None, :],
        "g_pre_mix": g_pre_mix[None, :], "g_post_mix": g_post_mix[None, :],
        "g_pre_ffn2": g_pre_ffn2[None, :], "g_post_ffn2": g_post_ffn2[None, :],
    }


ZXD_SPLITS = ((0, B_D_INNER), (B_D_INNER, B_D_INNER + B_CONV_DIM), (B_D_INNER + B_CONV_DIM, B_D_INNER + B_CONV_DIM + DT_PAD))


def _prompt_layer(x, prm, tm, tq):
    b, s, _ = x.shape
    t = b * s
    h1, hn = _ffn_call(x.reshape(t, D_MODEL), prm["g_pre_ffn1"], prm["g_post_ffn1"], prm["g_pre_mix"], *prm["ffn1"], FFN_ROW_TILE)
    qkvs = _proj_qkv_call(hn, prm["w_qkv"], b, s, PROJ_ROW_TILE)
    z, xbc, dt = _proj_call(hn, prm["w_zxd"], ZXD_SPLITS, (BF16, F32, F32), PROJ_ROW_TILE, "proj_zxd")

    kv_tails = []
    for g, (window, _) in enumerate(A_GROUPS):
        rows = min(window, s)
        kvt = _proj_t_call(hn, prm["w_kv_t"][g], b, s, rows, min(rows, tm), f"proj_kv_g{g}")
        kvt = kvt.reshape(b, 2, A_HEADS, A_HEAD_DIM, rows)
        kv_tails.append(jnp.transpose(kvt, (0, 4, 1, 2, 3))[None])

    oes = [_attn_prompt_call(qkvs[g], tq, f"attn_prompt_g{g}") for g in range(N_A_GROUPS)]
    oes[0] = oes[0].reshape(t, OE_COLS)
    yb, conv_out, ssm_out = _ssd_call(
        xbc.reshape(b, s, B_CONV_DIM), z.reshape(b, s, B_D_INNER), dt.reshape(b, s, DT_PAD),
        jnp.zeros((b, B_CONV - 1, B_CONV_DIM), F32), jnp.zeros((b, B_HEADS, B_HEAD_DIM, B_D_STATE), F32),
        prm, B_CHUNK)
    h2 = _merge_call(oes, yb.reshape(t, B_D_INNER), hn, h1, prm, tm)
    (y,) = _ffn_call(h2, prm["g_pre_ffn2"], prm["g_post_ffn2"], None, *prm["ffn2"], FFN_ROW_TILE)
    return y.reshape(b, s, D_MODEL), kv_tails, conv_out[None], ssm_out[None]


def _sample_layer(x, caches, conv_state, ssm_state, prm):
    b, l, _ = x.shape
    t = b * l
    h1, hn = _ffn_call(x.reshape(t, D_MODEL), prm["g_pre_ffn1"], prm["g_post_ffn1"], prm["g_pre_mix"], *prm["ffn1"], t)
    (qkv,) = _proj_call(hn, prm["w_qkv"], ((0, OFF_Z),), (F32,), t, "proj_qkv_s")
    z, xbc, dt = _proj_call(hn, prm["w_zxd"], ZXD_SPLITS, (BF16, F32, F32), t, "proj_zxd_s")

    qkv5 = qkv.reshape(b, l, N_A_GROUPS, 3, A_OUT)
    new_caches, oes = [], []
    for g in range(N_A_GROUPS):
        newbuf, o, lse = _attn_sample_call(caches[g], qkv5[:, :, g, 0], qkv5[:, :, g, 1], qkv5[:, :, g, 2], g)
        new_caches.append(newbuf[None])
        oes.append(jnp.concatenate([o, lse], axis=-1).reshape(t, OE_COLS))

    pad = lambda a: jnp.pad(a.reshape(b, l, a.shape[-1]), ((0, 0), (0, B_CHUNK - l), (0, 0)))
    yb, conv_out, ssm_out = _ssd_call(pad(xbc), pad(z), pad(dt), conv_state, ssm_state, prm, l)
    h2 = _merge_call(oes, yb[:, :l].reshape(t, B_D_INNER), hn, h1, prm, t)
    (y,) = _ffn_call(h2, prm["g_pre_ffn2"], prm["g_post_ffn2"], None, *prm["ffn2"], t)
    return y.reshape(b, l, D_MODEL), new_caches, conv_out[None], ssm_out[None]


PROMPT_ROW_TILE = 512
FFN_ROW_TILE = 1024
PROJ_ROW_TILE = 1024
PROMPT_QUERY_TILE = 512


def kernel(x_prompt, x_sample, cache_kv_w128, cache_kv_w512, cache_kv_w2048, state_conv, state_ssm, w_in, conv_w, conv_b, dt_bias, a_log, d_skip, ssd_norm_w, w_branch_a, w_branch_b, w_out, ffn1_gate, ffn1_up, ffn1_down, ffn2_gate, ffn2_up, ffn2_down, g_pre_ffn1, g_post_ffn1, g_pre_mix, g_post_mix, g_pre_ffn2, g_post_ffn2):
    assert w_in.shape[0] == 1, "single-layer trunk"
    prm = _prepare(*(p[0] for p in (
        w_in, conv_w, conv_b, dt_bias, a_log, d_skip, ssd_norm_w, w_branch_a, w_branch_b, w_out,
        ffn1_gate, ffn1_up, ffn1_down, ffn2_gate, ffn2_up, ffn2_down,
        g_pre_ffn1, g_post_ffn1, g_pre_mix, g_post_mix, g_pre_ffn2, g_post_ffn2)))
    y_p, kv_p, conv_p, ssm_p = _prompt_layer(x_prompt, prm, PROMPT_ROW_TILE, PROMPT_QUERY_TILE)
    y_s, kv_s, conv_s, ssm_s = _sample_layer(
        x_sample, (cache_kv_w128[0], cache_kv_w512[0], cache_kv_w2048[0]), state_conv[0], state_ssm[0], prm)
    return (y_p, y_s, kv_p[0], kv_p[1], kv_p[2], conv_p, ssm_p, kv_s[0], kv_s[1], kv_s[2], conv_s, ssm_s)
```
